```python
import jax
import jax.numpy as jnp
from jax import lax
import numpy as np

D_MODEL = 4096
BATCH = 1
SEQ = 8192
DEPTH = 2

N_MIXERS = 2
D_FF = 256 * (-(-(8 * D_MODEL) // (3 * 256)))
LN_EPS = 1e-5
RMS_EPS = 1e-6
ALPHA = float((2 * DEPTH) ** 0.25)
BETA = float((8 * DEPTH) ** -0.25)
HG_HEAD_DIM = 128
HG_HEADS = D_MODEL // HG_HEAD_DIM
HG_CHUNK = 64
MB_HEAD_DIM = 128
MB_HEADS = D_MODEL // MB_HEAD_DIM
MB_BLOCK = 256
MB_TOPK = 3
MB_QCHUNK = 16

kernel_name = 'hybrid_hgrn2_moba_macaron_deepnorm'


def layer_norm(x, g, b):
    xf = x.astype(jnp.float32)
    mu = jnp.mean(xf, axis=-1, keepdims=True)
    var = jnp.mean(jnp.square(xf - mu), axis=-1, keepdims=True)
    y = (xf - mu) * lax.rsqrt(var + LN_EPS)
    return (y * g.astype(jnp.float32) + b.astype(jnp.float32)).astype(x.dtype)


def swiglu_ffn(x, w_in, w_out):
    gate, up = jnp.split(x @ w_in, 2, axis=-1)
    return ((jax.nn.silu(gate) * up) @ w_out).astype(x.dtype)


def hgrn_lower_bound(lb_logits, layer_idx):
    p = jax.nn.softmax(lb_logits.astype(jnp.float32), axis=0)
    return jnp.cumsum(p, axis=0)[layer_idx]


def hgrn2_mixer(x, w_in, norm_g, w_out, lower_bound):
    B, S, D = x.shape
    H, dk, C = HG_HEADS, HG_HEAD_DIM, HG_CHUNK
    Sp = -(-S // C) * C
    n = Sp // C
    xp = jnp.pad(x, ((0, 0), (0, Sp - S), (0, 0)))
    q, f, v, g = jnp.split((xp @ w_in).astype(jnp.float32), 4, axis=-1)
    q = jax.nn.silu(q)
    forget = lower_bound + (1.0 - lower_bound) * jax.nn.sigmoid(f)
    k = 1.0 - forget
    log_f = jnp.log(forget)

    def chunks(t):
        return t.reshape(B, n, C, H, dk).transpose(1, 0, 3, 2, 4)

    causal = jnp.tril(jnp.ones((C, C), dtype=bool))[:, :, None]

    def step(state, inp):
        qc, kc, vc, lc = inp
        cum = jnp.cumsum(lc, axis=2)
        rel = cum[:, :, :, None, :] - cum[:, :, None, :, :]
        decay = jnp.exp(jnp.where(causal, rel, -jnp.inf))
        scores = jnp.einsum('bhtsd,bhsd->bhts', qc[:, :, :, None, :] * decay, kc)
        o = (jnp.einsum('bhts,bhse->bhte', scores, vc)
             + jnp.einsum('bhtd,bhde->bhte', qc * jnp.exp(cum), state))
        last = cum[:, :, -1:, :]
        state = (state * jnp.exp(last)[:, :, 0, :, None]
                 + jnp.einsum('bhsd,bhse->bhde', kc * jnp.exp(last - cum), vc))
        return state, o

    s0 = jnp.zeros((B, H, dk, dk), jnp.float32)
    _, o = lax.scan(step, s0, (chunks(q), chunks(k), chunks(v), chunks(log_f)))
    o = o.transpose(1, 0, 3, 2, 4).reshape(B, Sp, H, dk)
    o = o * lax.rsqrt(jnp.mean(jnp.square(o), axis=-1, keepdims=True) + RMS_EPS)
    o = o * norm_g.astype(jnp.float32).reshape(H, dk)
    o = o.reshape(B, Sp, D) * jax.nn.silu(g)
    return (o[:, :S] @ w_out.astype(jnp.float32)).astype(x.dtype)


def moba_mixer(x, w_in, w_out):
    B, S, D = x.shape
    H, dh, L, QC = MB_HEADS, MB_HEAD_DIM, MB_BLOCK, MB_QCHUNK
    Sp = -(-S // L) * L
    nb, nc = Sp // L, Sp // QC
    n_sel = min(MB_TOPK, nb)
    scale = dh ** -0.5
    xp = jnp.pad(x, ((0, 0), (0, Sp - S), (0, 0)))
    qkv = (xp @ w_in).astype(jnp.float32).reshape(B, Sp, 3, H, dh).transpose(2, 0, 3, 1, 4)
    q, k, v = qkv[0], qkv[1], qkv[2]
    kb = k.reshape(B, H, nb, L, dh)
    vb = v.reshape(B, H, nb, L, dh)

    k_mean = jnp.mean(kb, axis=3)
    gate = jnp.einsum('bhtd,bhnd->bhtn', q, k_mean)
    q_blk = jnp.arange(Sp) // L
    fully_past = jnp.arange(nb)[None, :] < q_blk[:, None]
    gate = jnp.where(fully_past, gate, -jnp.inf)
    _, sel = lax.top_k(gate, n_sel)

    q_c = q.reshape(B, H, nc, QC, dh).transpose(2, 0, 1, 3, 4)
    sel_c = sel.reshape(B, H, nc, QC, n_sel).transpose(2, 0, 1, 3, 4)
    b_idx = jnp.arange(B)[:, None, None, None]
    h_idx = jnp.arange(H)[None, :, None, None]

    def attend(args):
        c, qq, ss = args
        j = (c * QC) // L
        k_sel = kb[b_idx, h_idx, ss]
        v_sel = vb[b_idx, h_idx, ss]
        s_sel = jnp.einsum('bhqd,bhqnkd->bhqnk', qq, k_sel) * scale
        s_sel = jnp.where((ss < j)[..., None], s_sel, -jnp.inf)
        k_own = lax.dynamic_index_in_dim(kb, j, axis=2, keepdims=False)
        v_own = lax.dynamic_index_in_dim(vb, j, axis=2, keepdims=False)
        s_own = jnp.einsum('bhqd,bhkd->bhqk', qq, k_own) * scale
        q_pos = c * QC + jnp.arange(QC)
        k_pos = j * L + jnp.arange(L)
        s_own = jnp.where(k_pos[None, :] <= q_pos[:, None], s_own, -jnp.inf)
        s_all = jnp.concatenate([s_sel.reshape(B, H, QC, n_sel * L), s_own], axis=-1)
        p = jax.nn.softmax(s_all, axis=-1)
        p_sel = p[..., :n_sel * L].reshape(B, H, QC, n_sel, L)
        p_own = p[..., n_sel * L:]
        return (jnp.einsum('bhqnk,bhqnkd->bhqd', p_sel, v_sel)
                + jnp.einsum('bhqk,bhkd->bhqd', p_own, v_own))

    o = lax.map(attend, (jnp.arange(nc), q_c, sel_c))
    o = o.transpose(1, 0, 3, 2, 4).reshape(B, Sp, H * dh)[:, :S]
    return (o @ w_out.astype(jnp.float32)).astype(x.dtype)


def setup_inputs(seed: int = 0) -> dict:
    key = jax.random.key(seed)
    ks = iter(jax.random.split(key, 64))
    f32 = jnp.float32
    D, F = D_MODEL, D_FF

    def dense(shape, fan_in, scale=1.0):
        return jax.random.normal(next(ks), shape, f32) * (scale * fan_in ** -0.5)

    def gain(n):
        return 1.0 + 0.02 * jax.random.normal(next(ks), (n,), f32)

    def bias(n):
        return 0.02 * jax.random.normal(next(ks), (n,), f32)

    inp = {}
    inp['x'] = jax.random.normal(next(ks), (BATCH, SEQ, D), f32)
    inp['lb_logits'] = jax.random.normal(next(ks), (DEPTH + 1, D), f32)
    inp['l0_ffn1_in'] = dense((D, 2 * F), D)
    inp['l0_ffn1_out'] = dense((F, D), F, BETA)
    inp['l0_ln1_g'] = gain(D)
    inp['l0_ln1_b'] = bias(D)
    inp['l0_hg_in'] = dense((D, 4 * D), D)
    inp['l0_hg_norm_g'] = gain(D)
    inp['l0_hg_out'] = dense((D, D), D, BETA)
    inp['l0_ln2_g'] = gain(D)
    inp['l0_ln2_b'] = bias(D)
    inp['l0_ffn2_in'] = dense((D, 2 * F), D)
    inp['l0_ffn2_out'] = dense((F, D), F, BETA)
    inp['l0_ln3_g'] = gain(D)
    inp['l0_ln3_b'] = bias(D)
    inp['l1_ffn1_in'] = dense((D, 2 * F), D)
    inp['l1_ffn1_out'] = dense((F, D), F, BETA)
    inp['l1_ln1_g'] = gain(D)
    inp['l1_ln1_b'] = bias(D)
    inp['l1_mb_in'] = dense((D, 3 * D), D)
    inp['l1_mb_out'] = dense((D, D), D, BETA)
    inp['l1_ln2_g'] = gain(D)
    inp['l1_ln2_b'] = bias(D)
    inp['l1_ffn2_in'] = dense((D, 2 * F), D)
    inp['l1_ffn2_out'] = dense((F, D), F, BETA)
    inp['l1_ln3_g'] = gain(D)
    inp['l1_ln3_b'] = bias(D)
    return inp


def reference(x, lb_logits,
              l0_ffn1_in, l0_ffn1_out, l0_ln1_g, l0_ln1_b, l0_hg_in, l0_hg_norm_g, l0_hg_out,
              l0_ln2_g, l0_ln2_b, l0_ffn2_in, l0_ffn2_out, l0_ln3_g, l0_ln3_b,
              l1_ffn1_in, l1_ffn1_out, l1_ln1_g, l1_ln1_b, l1_mb_in, l1_mb_out,
              l1_ln2_g, l1_ln2_b, l1_ffn2_in, l1_ffn2_out, l1_ln3_g, l1_ln3_b):
    ffn_pre = ((l0_ffn1_in, l0_ffn1_out), (l1_ffn1_in, l1_ffn1_out))
    ffn_post = ((l0_ffn2_in, l0_ffn2_out), (l1_ffn2_in, l1_ffn2_out))
    ln_pre = ((l0_ln1_g, l0_ln1_b), (l1_ln1_g, l1_ln1_b))
    ln_mix = ((l0_ln2_g, l0_ln2_b), (l1_ln2_g, l1_ln2_b))
    ln_post = ((l0_ln3_g, l0_ln3_b), (l1_ln3_g, l1_ln3_b))
    mixer_of_layer = (
        lambda t: hgrn2_mixer(t, l0_hg_in, l0_hg_norm_g, l0_hg_out, hgrn_lower_bound(lb_logits, 0)),
        lambda t: moba_mixer(t, l1_mb_in, l1_mb_out),
    )
    h = x
    for i in range(DEPTH):
        h = layer_norm(ALPHA * h + 0.5 * swiglu_ffn(h, *ffn_pre[i]), *ln_pre[i])
        h = layer_norm(ALPHA * h + mixer_of_layer[i](h), *ln_mix[i])
        h = layer_norm(ALPHA * h + 0.5 * swiglu_ffn(h, *ffn_post[i]), *ln_post[i])
    return h
```

```python
import functools

import jax
import jax.numpy as jnp
from jax import lax
from jax.experimental import pallas as pl
from jax.experimental.pallas import tpu as pltpu

F32 = jnp.float32
BF16 = jnp.bfloat16

DEPTH = 2
ALPHA = float((2 * DEPTH) ** 0.25)
LN_EPS = 1e-5
RMS_EPS = 1e-6
HEAD_DIM = 128
SUBLANES = 8
MB_BLOCK = 256
MB_TOPK = 3

VMEM_LIMIT_BYTES = 56 * 1024 * 1024

FFN_TM = 512
FFN_TF = 256
PROJ_TM = 512
PROJ_TN = 512
OUT_TM = 512
OUT_TK = 512
HG_TB = 256
HG_CHUNK = 64
HG_HB = 4
HG_SAFE_LOG = 60.0


def _params(*sem):
    return pltpu.CompilerParams(dimension_semantics=sem, vmem_limit_bytes=VMEM_LIMIT_BYTES)


LN_ROWS = 128
ACC_TN = 1024


def _layer_norm_inplace(o_ref, g_ref, b_ref):
    g = g_ref[...]
    b = b_ref[...]

    def body(r, carry):
        rows = pl.ds(pl.multiple_of(r * LN_ROWS, LN_ROWS), LN_ROWS)
        y = o_ref[rows, :]
        mu = jnp.mean(y, axis=-1, keepdims=True)
        yc = y - mu
        var = jnp.mean(yc * yc, axis=-1, keepdims=True)
        o_ref[rows, :] = yc * lax.rsqrt(var + LN_EPS) * g + b
        return carry

    lax.fori_loop(0, o_ref.shape[0] // LN_ROWS, body, 0)


def _accumulate_dot(o_ref, a, w_ref):
    for c in range(o_ref.shape[1] // ACC_TN):
        cols = slice(c * ACC_TN, (c + 1) * ACC_TN)
        o_ref[:, cols] += jnp.dot(a, w_ref[:, cols], preferred_element_type=F32)


def _ffn_ln_kernel(x_ref, wg_ref, wu_ref, wo_ref, g_ref, b_ref, o_ref, xb_ref):
    j = pl.program_id(1)

    @pl.when(j == 0)
    def _():
        x = x_ref[...]
        xb_ref[...] = x.astype(BF16)
        o_ref[...] = ALPHA * x

    xb = xb_ref[...]
    gate = jnp.dot(xb, wg_ref[...], preferred_element_type=F32)
    up = jnp.dot(xb, wu_ref[...], preferred_element_type=F32)
    h = (0.5 * gate * jax.nn.sigmoid(gate) * up).astype(BF16)
    _accumulate_dot(o_ref, h, wo_ref)

    @pl.when(j == pl.num_programs(1) - 1)
    def _():
        _layer_norm_inplace(o_ref, g_ref, b_ref)


def _ffn_ln(x, w_in, w_out, g, b):
    S, D = x.shape
    F = w_out.shape[0]
    nf = F // FFN_TF
    return pl.pallas_call(
        _ffn_ln_kernel,
        grid=(S // FFN_TM, nf),
        in_specs=[
            pl.BlockSpec((FFN_TM, D), lambda i, j: (i, 0)),
            pl.BlockSpec((D, FFN_TF), lambda i, j: (0, j)),
            pl.BlockSpec((D, FFN_TF), lambda i, j: (0, j + nf)),
            pl.BlockSpec((FFN_TF, D), lambda i, j: (j, 0)),
            pl.BlockSpec((1, D), lambda i, j: (0, 0)),
            pl.BlockSpec((1, D), lambda i, j: (0, 0)),
        ],
        out_specs=pl.BlockSpec((FFN_TM, D), lambda i, j: (i, 0)),
        out_shape=jax.ShapeDtypeStruct((S, D), F32),
        scratch_shapes=[pltpu.VMEM((FFN_TM, D), BF16)],
        compiler_params=_params("parallel", "arbitrary"),
        name="ffn_ln",
    )(x, w_in, w_in, w_out, g.reshape(1, D), b.reshape(1, D))


def _proj_kernel(x_ref, w_ref, o_ref, xb_ref):
    @pl.when(pl.program_id(1) == 0)
    def _():
        xb_ref[...] = x_ref[...].astype(BF16)

    o_ref[...] = jnp.dot(xb_ref[...], w_ref[...], preferred_element_type=F32).astype(o_ref.dtype)


def _proj(x, w, out_dtype):
    S, D = x.shape
    N = w.shape[1]
    return pl.pallas_call(
        _proj_kernel,
        grid=(S // PROJ_TM, N // PROJ_TN),
        in_specs=[
            pl.BlockSpec((PROJ_TM, D), lambda i, j: (i, 0)),
            pl.BlockSpec((D, PROJ_TN), lambda i, j: (0, j)),
        ],
        out_specs=pl.BlockSpec((PROJ_TM, PROJ_TN), lambda i, j: (i, j)),
        out_shape=jax.ShapeDtypeStruct((S, N), out_dtype),
        scratch_shapes=[pltpu.VMEM((PROJ_TM, D), BF16)],
        compiler_params=_params("parallel", "arbitrary"),
        name="proj",
    )(x, w)


def _out_ln_kernel(x_ref, a_ref, w_ref, g_ref, b_ref, o_ref):
    k = pl.program_id(1)

    @pl.when(k == 0)
    def _():
        o_ref[...] = ALPHA * x_ref[...]

    _accumulate_dot(o_ref, a_ref[...], w_ref)

    @pl.when(k == pl.num_programs(1) - 1)
    def _():
        _layer_norm_inplace(o_ref, g_ref, b_ref)


def _out_ln(x, a, w, g, b):
    S, D = x.shape
    K = a.shape[1]
    return pl.pallas_call(
        _out_ln_kernel,
        grid=(S // OUT_TM, K // OUT_TK),
        in_specs=[
            pl.BlockSpec((OUT_TM, D), lambda i, k: (i, 0)),
            pl.BlockSpec((OUT_TM, OUT_TK), lambda i, k: (i, k)),
            pl.BlockSpec((OUT_TK, D), lambda i, k: (k, 0)),
            pl.BlockSpec((1, D), lambda i, k: (0, 0)),
            pl.BlockSpec((1, D), lambda i, k: (0, 0)),
        ],
        out_specs=pl.BlockSpec((OUT_TM, D), lambda i, k: (i, 0)),
        out_shape=jax.ShapeDtypeStruct((S, D), F32),
        compiler_params=_params("parallel", "arbitrary"),
        name="out_ln",
    )(x, a, w, g.reshape(1, D), b.reshape(1, D))


def _nt_dot(a, b):
    return lax.dot_general(a, b, (((1,), (1,)), ((), ())), preferred_element_type=F32)


def _tn_dot(a, b):
    return lax.dot_general(a, b, (((0,), (0,)), ((), ())), preferred_element_type=F32)


def _hgrn_kernel(q_ref, f_ref, v_ref, g_ref, lb_ref, ng_ref, o_ref,
                 st_ref, cum_ref, qs_ref, ks_ref, oi_ref):
    TB, W = q_ref.shape
    C = HG_CHUNK
    nchunk = TB // C
    nhead = W // HEAD_DIM

    @pl.when(pl.program_id(1) == 0)
    def _():
        st_ref[...] = jnp.zeros_like(st_ref)

    lg = lb_ref[...]
    ex = jnp.exp(lg - jnp.max(lg, axis=0, keepdims=True))
    lb = ex[0:1, :] / jnp.sum(ex, axis=0, keepdims=True)
    qv = q_ref[...]
    q = qv * jax.nn.sigmoid(qv)
    forget = lb + (1.0 - lb) * jax.nn.sigmoid(f_ref[...])
    k = 1.0 - forget
    logf = jnp.log(forget)

    r = lax.broadcasted_iota(jnp.int32, (TB, TB), 0)
    c = lax.broadcasted_iota(jnp.int32, (TB, TB), 1)
    tri = jnp.where((c <= r) & (c // C == r // C), 1.0, 0.0).astype(BF16)
    hi = logf.astype(BF16)
    r1 = logf - hi.astype(F32)
    mid = r1.astype(BF16)
    lo = (r1 - mid.astype(F32)).astype(BF16)
    cum = (jnp.dot(tri, hi, preferred_element_type=F32)
           + jnp.dot(tri, mid, preferred_element_type=F32)
           + jnp.dot(tri, lo, preferred_element_type=F32))

    cum_ref[...] = cum
    qs_ref[...] = q
    ks_ref[...] = k

    bound = jnp.zeros((1, W), F32)
    for ci in range(nchunk):
        mrow = cum[ci * C + C // 2 - 1: ci * C + C // 2, :]
        lrow = cum[ci * C + C - 1: ci * C + C, :]
        bound = jnp.maximum(bound, jnp.maximum(-mrow, mrow - lrow))
    safe = jnp.max(bound) <= HG_SAFE_LOG

    tr = lax.broadcasted_iota(jnp.int32, (C, C), 0)
    ts = lax.broadcasted_iota(jnp.int32, (C, C), 1)
    causal = ts <= tr

    @pl.when(safe)
    def _():
        for ci in range(nchunk):
            rows = slice(ci * C, (ci + 1) * C)
            for h in range(nhead):
                lanes = slice(h * HEAD_DIM, (h + 1) * HEAD_DIM)
                cu = cum[rows, lanes]
                m = cu[C // 2 - 1: C // 2, :]
                qd = (q[rows, lanes] * jnp.exp(cu - m)).astype(BF16)
                kd = (k[rows, lanes] * jnp.exp(m - cu)).astype(BF16)
                a = jnp.where(causal, _nt_dot(qd, kd), 0.0).astype(BF16)
                oi_ref[rows, lanes] = jnp.dot(a, v_ref[rows, lanes].astype(BF16),
                                              preferred_element_type=F32)

    @pl.when(jnp.logical_not(safe))
    def _():
        sidx = lax.broadcasted_iota(jnp.int32, (C, W), 0)
        for ci in range(nchunk):
            rows = slice(ci * C, (ci + 1) * C)
            cu = cum_ref[rows, :]
            kc = ks_ref[rows, :]
            vc = v_ref[rows, :]

            def rows_body(t8, carry, ci=ci, cu=cu, kc=kc, vc=vc):
                base = pl.multiple_of(ci * C + t8 * SUBLANES, SUBLANES)
                c8 = cum_ref[pl.ds(base, SUBLANES), :]
                q8 = qs_ref[pl.ds(base, SUBLANES), :]
                out_rows = []
                for r in range(SUBLANES):
                    t = t8 * SUBLANES + r
                    dec = jnp.exp(jnp.where(sidx <= t, c8[r:r + 1, :] - cu, -jnp.inf))
                    prod = q8[r:r + 1, :] * dec * kc
                    heads = []
                    for h in range(nhead):
                        lanes = slice(h * HEAD_DIM, (h + 1) * HEAD_DIM)
                        sc = jnp.sum(prod[:, lanes], axis=1, keepdims=True)
                        heads.append(jnp.sum(sc * vc[:, lanes], axis=0, keepdims=True))
                    out_rows.append(jnp.concatenate(heads, axis=1))
                oi_ref[pl.ds(base, SUBLANES), :] = jnp.concatenate(out_rows, axis=0)
                return carry

            lax.fori_loop(0, C // SUBLANES, rows_body, 0)

    gv = g_ref[...]
    og = ng_ref[...] * (gv * jax.nn.sigmoid(gv))
    for h in range(nhead):
        lanes = slice(h * HEAD_DIM, (h + 1) * HEAD_DIM)
        st = st_ref[h]
        for ci in range(nchunk):
            rows = slice(ci * C, (ci + 1) * C)
            cu = cum[rows, lanes]
            last = cu[C - 1: C, :]
            qe = (q[rows, lanes] * jnp.exp(cu)).astype(BF16)
            o = oi_ref[rows, lanes] + _nt_dot(qe, st.astype(BF16))
            ke = (k[rows, lanes] * jnp.exp(last - cu)).astype(BF16)
            st = st * jnp.exp(last) + _tn_dot(v_ref[rows, lanes].astype(BF16), ke)
            o = o * lax.rsqrt(jnp.mean(o * o, axis=-1, keepdims=True) + RMS_EPS)
            o_ref[rows, lanes] = (o * og[rows, lanes]).astype(o_ref.dtype)
        st_ref[h] = st


def _hgrn_core(proj, lb_logits, norm_g):
    S, D4 = proj.shape
    D = D4 // 4
    W = HG_HB * HEAD_DIM
    ng = D // W
    return pl.pallas_call(
        _hgrn_kernel,
        grid=(ng, S // HG_TB),
        in_specs=[
            pl.BlockSpec((HG_TB, W), lambda hg, t: (t, hg)),
            pl.BlockSpec((HG_TB, W), lambda hg, t: (t, hg + ng)),
            pl.BlockSpec((HG_TB, W), lambda hg, t: (t, hg + 2 * ng)),
            pl.BlockSpec((HG_TB, W), lambda hg, t: (t, hg + 3 * ng)),
            pl.BlockSpec((lb_logits.shape[0], W), lambda hg, t: (0, hg)),
            pl.BlockSpec((1, W), lambda hg, t: (0, hg)),
        ],
        out_specs=pl.BlockSpec((HG_TB, W), lambda hg, t: (t, hg)),
        out_shape=jax.ShapeDtypeStruct((S, D), BF16),
        scratch_shapes=[
            pltpu.VMEM((HG_HB, HEAD_DIM, HEAD_DIM), F32),
            pltpu.VMEM((HG_TB, W), F32),
            pltpu.VMEM((HG_TB, W), F32),
            pltpu.VMEM((HG_TB, W), F32),
            pltpu.VMEM((HG_TB, W), F32),
        ],
        compiler_params=_params("parallel", "arbitrary"),
        name="hgrn_core",
    )(proj, proj, proj, proj, lb_logits.astype(F32), norm_g.reshape(1, D))


def _moba_kernel(q_ref, k_ref, v_ref, o_ref, km_ref, sel_ref):
    L = MB_BLOCK
    S = k_ref.shape[0]
    nb = S // L
    i = pl.program_id(1)
    scale = HEAD_DIM ** -0.5

    @pl.when(i == 0)
    def _():
        def mean_body(n, carry):
            blk = k_ref[pl.ds(pl.multiple_of(n * L, L), L), :].astype(F32)
            km_ref[pl.ds(n, 1), :] = jnp.mean(blk, axis=0, keepdims=True)
            return carry
        lax.fori_loop(0, nb, mean_body, 0)

    q = q_ref[...]

    gate = _nt_dot(km_ref[...].astype(BF16), q)
    nidx = lax.broadcasted_iota(jnp.int32, (nb, L), 0)
    gate = jnp.where(nidx < i, gate, -jnp.inf)
    sel = jnp.zeros((nb, L), F32)
    for _ in range(min(MB_TOPK, nb)):
        mx = jnp.max(gate, axis=0, keepdims=True)
        first = jnp.min(jnp.where(gate == mx, nidx, nb), axis=0, keepdims=True)
        pick = (nidx == first) & (mx > -jnp.inf)
        sel = jnp.where(pick, 1.0, sel)
        gate = jnp.where(pick, -jnp.inf, gate)
    sel_ref[...] = sel

    own = pl.ds(pl.multiple_of(i * L, L), L)
    st = _nt_dot(k_ref[own, :], q) * scale
    kpos = lax.broadcasted_iota(jnp.int32, (L, L), 0)
    qpos = lax.broadcasted_iota(jnp.int32, (L, L), 1)
    st = jnp.where(kpos <= qpos, st, -jnp.inf)
    m0 = jnp.max(st, axis=0, keepdims=True)
    p = jnp.exp(st - m0)
    l0 = jnp.sum(p, axis=0, keepdims=True)
    acc0 = _tn_dot(v_ref[own, :], p.astype(BF16))

    def blk_body(n, carry):
        m, l, acc = carry
        rows = pl.ds(pl.multiple_of(n * L, L), L)
        s = _nt_dot(k_ref[rows, :], q) * scale
        s = jnp.where(sel_ref[pl.ds(n, 1), :] > 0.0, s, -jnp.inf)
        m_new = jnp.maximum(m, jnp.max(s, axis=0, keepdims=True))
        alpha = jnp.exp(m - m_new)
        p = jnp.exp(s - m_new)
        l = alpha * l + jnp.sum(p, axis=0, keepdims=True)
        acc = alpha * acc + _tn_dot(v_ref[rows, :], p.astype(BF16))
        return m_new, l, acc

    m, l, acc = lax.fori_loop(0, i, blk_body, (m0, l0, acc0))
    o_ref[...] = (acc / l).T.astype(o_ref.dtype)


def _moba_core(qkv):
    S, D3 = qkv.shape
    D = D3 // 3
    H = D // HEAD_DIM
    L = MB_BLOCK
    nb = S // L
    return pl.pallas_call(
        _moba_kernel,
        grid=(H, nb),
        in_specs=[
            pl.BlockSpec((L, HEAD_DIM), lambda h, i: (i, h)),
            pl.BlockSpec((S, HEAD_DIM), lambda h, i: (0, H + h)),
            pl.BlockSpec((S, HEAD_DIM), lambda h, i: (0, 2 * H + h)),
        ],
        out_specs=pl.BlockSpec((L, HEAD_DIM), lambda h, i: (i, h)),
        out_shape=jax.ShapeDtypeStruct((S, D), BF16),
        scratch_shapes=[
            pltpu.VMEM((nb, HEAD_DIM), F32),
            pltpu.VMEM((nb, L), F32),
        ],
        compiler_params=_params("parallel", "arbitrary"),
        name="moba_core",
    )(qkv, qkv, qkv)


def kernel(x, lb_logits, l0_ffn1_in, l0_ffn1_out, l0_ln1_g, l0_ln1_b, l0_hg_in, l0_hg_norm_g, l0_hg_out, l0_ln2_g, l0_ln2_b, l0_ffn2_in, l0_ffn2_out, l0_ln3_g, l0_ln3_b, l1_ffn1_in, l1_ffn1_out, l1_ln1_g, l1_ln1_b, l1_mb_in, l1_mb_out, l1_ln2_g, l1_ln2_b, l1_ffn2_in, l1_ffn2_out, l1_ln3_g, l1_ln3_b):
    B, S, D = x.shape
    assert B == 1 and S % MB_BLOCK == 0 and S % FFN_TM == 0
    bf = lambda w: w.astype(BF16)
    h = x.reshape(S, D)

    h = _ffn_ln(h, bf(l0_ffn1_in), bf(l0_ffn1_out), l0_ln1_g, l0_ln1_b)
    proj = _proj(h, bf(l0_hg_in), F32)
    mix = _hgrn_core(proj, lb_logits, l0_hg_norm_g)
    h = _out_ln(h, mix, bf(l0_hg_out), l0_ln2_g, l0_ln2_b)
    h = _ffn_ln(h, bf(l0_ffn2_in), bf(l0_ffn2_out), l0_ln3_g, l0_ln3_b)

    h = _ffn_ln(h, bf(l1_ffn1_in), bf(l1_ffn1_out), l1_ln1_g, l1_ln1_b)
    qkv = _proj(h, bf(l1_mb_in), BF16)
    mix = _moba_core(qkv)
    h = _out_ln(h, mix, bf(l1_mb_out), l1_ln2_g, l1_ln2_b)
    h = _ffn_ln(h, bf(l1_ffn2_in), bf(l1_ffn2_out), l1_ln3_g, l1_ln3_b)
    return h.reshape(B, S, D)
```

```python
import functools

import jax
import jax.numpy as jnp
from jax import lax
from jax.experimental import pallas as pl
from jax.experimental.pallas import tpu as pltpu

F32 = jnp.float32
BF16 = jnp.bfloat16

DEPTH = 2
ALPHA = float((2 * DEPTH) ** 0.25)
LN_EPS = 1e-5
RMS_EPS = 1e-6
LOG2_E = 1.4426950408889634
HEAD_DIM = 128
SUBLANES = 8
MB_BLOCK = 256
MB_TOPK = 3
MB_UNROLL = 4

VMEM_LIMIT_BYTES = 56 * 1024 * 1024

FFN_TM = 512
FFN_TF = 256
PROJ_TM = 1024
PROJ_TN = 512
OUT_TM = 512
OUT_TK = 1024
HG_TB = 256
HG_CHUNK = 64
HG_HB = 4
HG_SAFE_LOG = 60.0


def _params(*sem):
    return pltpu.CompilerParams(dimension_semantics=sem, vmem_limit_bytes=VMEM_LIMIT_BYTES)


LN_ROWS = 128
ACC_TN = 1024


def _layer_norm_inplace(o_ref, g_ref, b_ref):
    g = g_ref[...]
    b = b_ref[...]

    def body(r, carry):
        rows = pl.ds(pl.multiple_of(r * LN_ROWS, LN_ROWS), LN_ROWS)
        y = o_ref[rows, :]
        mu = jnp.mean(y, axis=-1, keepdims=True)
        yc = y - mu
        var = jnp.mean(yc * yc, axis=-1, keepdims=True)
        o_ref[rows, :] = yc * lax.rsqrt(var + LN_EPS) * g + b
        return carry

    lax.fori_loop(0, o_ref.shape[0] // LN_ROWS, body, 0)


def _accumulate_dot(o_ref, a, w_ref):
    for c in range(o_ref.shape[1] // ACC_TN):
        cols = slice(c * ACC_TN, (c + 1) * ACC_TN)
        o_ref[:, cols] += jnp.dot(a, w_ref[:, cols], preferred_element_type=F32)


def _ffn_ln_kernel(x_ref, wg_ref, wu_ref, wo_ref, g_ref, b_ref, o_ref, xb_ref):
    j = pl.program_id(1)

    @pl.when(j == 0)
    def _():
        x = x_ref[...]
        xb_ref[...] = x.astype(BF16)
        o_ref[...] = ALPHA * x

    xb = xb_ref[...]
    gate = jnp.dot(xb, wg_ref[...], preferred_element_type=F32)
    up = jnp.dot(xb, wu_ref[...], preferred_element_type=F32)
    h = (0.5 * gate * jax.nn.sigmoid(gate) * up).astype(BF16)
    _accumulate_dot(o_ref, h, wo_ref)

    @pl.when(j == pl.num_programs(1) - 1)
    def _():
        _layer_norm_inplace(o_ref, g_ref, b_ref)


def _ffn_ln(x, w_in, w_out, g, b):
    S, D = x.shape
    F = w_out.shape[0]
    nf = F // FFN_TF
    return pl.pallas_call(
        _ffn_ln_kernel,
        grid=(S // FFN_TM, nf),
        in_specs=[
            pl.BlockSpec((FFN_TM, D), lambda i, j: (i, 0)),
            pl.BlockSpec((D, FFN_TF), lambda i, j: (0, j)),
            pl.BlockSpec((D, FFN_TF), lambda i, j: (0, j + nf)),
            pl.BlockSpec((FFN_TF, D), lambda i, j: (j, 0)),
            pl.BlockSpec((1, D), lambda i, j: (0, 0)),
            pl.BlockSpec((1, D), lambda i, j: (0, 0)),
        ],
        out_specs=pl.BlockSpec((FFN_TM, D), lambda i, j: (i, 0)),
        out_shape=jax.ShapeDtypeStruct((S, D), F32),
        scratch_shapes=[pltpu.VMEM((FFN_TM, D), BF16)],
        compiler_params=_params("parallel", "arbitrary"),
        name="ffn_ln",
    )(x, w_in, w_in, w_out, g.reshape(1, D), b.reshape(1, D))


def _proj_kernel(x_ref, w_ref, o_ref, xb_ref):
    @pl.when(pl.program_id(1) == 0)
    def _():
        xb_ref[...] = x_ref[...].astype(BF16)

    o_ref[...] = jnp.dot(xb_ref[...], w_ref[...], preferred_element_type=F32).astype(o_ref.dtype)


def _proj(x, w, out_dtype):
    S, D = x.shape
    N = w.shape[1]
    return pl.pallas_call(
        _proj_kernel,
        grid=(S // PROJ_TM, N // PROJ_TN),
        in_specs=[
            pl.BlockSpec((PROJ_TM, D), lambda i, j: (i, 0)),
            pl.BlockSpec((D, PROJ_TN), lambda i, j: (0, j)),
        ],
        out_specs=pl.BlockSpec((PROJ_TM, PROJ_TN), lambda i, j: (i, j)),
        out_shape=jax.ShapeDtypeStruct((S, N), out_dtype),
        scratch_shapes=[pltpu.VMEM((PROJ_TM, D), BF16)],
        compiler_params=_params("parallel", "arbitrary"),
        name="proj",
    )(x, w)


def _out_ln_kernel(x_ref, a_ref, w_ref, g_ref, b_ref, o_ref):
    k = pl.program_id(1)

    @pl.when(k == 0)
    def _():
        o_ref[...] = ALPHA * x_ref[...]

    _accumulate_dot(o_ref, a_ref[...], w_ref)

    @pl.when(k == pl.num_programs(1) - 1)
    def _():
        _layer_norm_inplace(o_ref, g_ref, b_ref)


def _out_ln(x, a, w, g, b):
    S, D = x.shape
    K = a.shape[1]
    return pl.pallas_call(
        _out_ln_kernel,
        grid=(S // OUT_TM, K // OUT_TK),
        in_specs=[
            pl.BlockSpec((OUT_TM, D), lambda i, k: (i, 0)),
            pl.BlockSpec((OUT_TM, OUT_TK), lambda i, k: (i, k)),
            pl.BlockSpec((OUT_TK, D), lambda i, k: (k, 0)),
            pl.BlockSpec((1, D), lambda i, k: (0, 0)),
            pl.BlockSpec((1, D), lambda i, k: (0, 0)),
        ],
        out_specs=pl.BlockSpec((OUT_TM, D), lambda i, k: (i, 0)),
        out_shape=jax.ShapeDtypeStruct((S, D), F32),
        compiler_params=_params("parallel", "arbitrary"),
        name="out_ln",
    )(x, a, w, g.reshape(1, D), b.reshape(1, D))


def _nt_dot(a, b):
    return lax.dot_general(a, b, (((1,), (1,)), ((), ())), preferred_element_type=F32)


def _tn_dot(a, b):
    return lax.dot_general(a, b, (((0,), (0,)), ((), ())), preferred_element_type=F32)


def _hgrn_kernel(q_ref, f_ref, v_ref, g_ref, lb_ref, ng_ref, o_ref,
                 st_ref, cum_ref, qs_ref, ks_ref, oi_ref):
    TB, W = q_ref.shape
    C = HG_CHUNK
    nchunk = TB // C
    nhead = W // HEAD_DIM

    @pl.when(pl.program_id(1) == 0)
    def _():
        st_ref[...] = jnp.zeros_like(st_ref)

    lg = lb_ref[...]
    ex = jnp.exp(lg - jnp.max(lg, axis=0, keepdims=True))
    lb = ex[0:1, :] / jnp.sum(ex, axis=0, keepdims=True)
    qv = q_ref[...]
    q = qv * jax.nn.sigmoid(qv)
    forget = lb + (1.0 - lb) * jax.nn.sigmoid(f_ref[...])
    k = 1.0 - forget
    logf = jnp.log(forget)

    r = lax.broadcasted_iota(jnp.int32, (TB, TB), 0)
    c = lax.broadcasted_iota(jnp.int32, (TB, TB), 1)
    tri = jnp.where((c <= r) & (c // C == r // C), 1.0, 0.0).astype(BF16)
    hi = logf.astype(BF16)
    r1 = logf - hi.astype(F32)
    mid = r1.astype(BF16)
    lo = (r1 - mid.astype(F32)).astype(BF16)
    cum = (jnp.dot(tri, hi, preferred_element_type=F32)
           + jnp.dot(tri, mid, preferred_element_type=F32)
           + jnp.dot(tri, lo, preferred_element_type=F32))

    cum_ref[...] = cum
    qs_ref[...] = q
    ks_ref[...] = k

    bound = jnp.zeros((1, W), F32)
    for ci in range(nchunk):
        mrow = cum[ci * C + C // 2 - 1: ci * C + C // 2, :]
        lrow = cum[ci * C + C - 1: ci * C + C, :]
        bound = jnp.maximum(bound, jnp.maximum(-mrow, mrow - lrow))
    safe = jnp.max(bound) <= HG_SAFE_LOG

    tr = lax.broadcasted_iota(jnp.int32, (C, C), 0)
    ts = lax.broadcasted_iota(jnp.int32, (C, C), 1)
    causal = ts <= tr

    @pl.when(safe)
    def _():
        for ci in range(nchunk):
            rows = slice(ci * C, (ci + 1) * C)
            for h in range(nhead):
                lanes = slice(h * HEAD_DIM, (h + 1) * HEAD_DIM)
                cu = cum[rows, lanes]
                m = cu[C // 2 - 1: C // 2, :]
                qd = (q[rows, lanes] * jnp.exp(cu - m)).astype(BF16)
                kd = (k[rows, lanes] * jnp.exp(m - cu)).astype(BF16)
                a = jnp.where(causal, _nt_dot(qd, kd), 0.0).astype(BF16)
                oi_ref[rows, lanes] = jnp.dot(a, v_ref[rows, lanes].astype(BF16),
                                              preferred_element_type=F32)

    @pl.when(jnp.logical_not(safe))
    def _():
        sidx = lax.broadcasted_iota(jnp.int32, (C, W), 0)
        for ci in range(nchunk):
            rows = slice(ci * C, (ci + 1) * C)
            cu = cum_ref[rows, :]
            kc = ks_ref[rows, :]
            vc = v_ref[rows, :]

            def rows_body(t8, carry, ci=ci, cu=cu, kc=kc, vc=vc):
                base = pl.multiple_of(ci * C + t8 * SUBLANES, SUBLANES)
                c8 = cum_ref[pl.ds(base, SUBLANES), :]
                q8 = qs_ref[pl.ds(base, SUBLANES), :]
                out_rows = []
                for r in range(SUBLANES):
                    t = t8 * SUBLANES + r
                    dec = jnp.exp(jnp.where(sidx <= t, c8[r:r + 1, :] - cu, -jnp.inf))
                    prod = q8[r:r + 1, :] * dec * kc
                    heads = []
                    for h in range(nhead):
                        lanes = slice(h * HEAD_DIM, (h + 1) * HEAD_DIM)
                        sc = jnp.sum(prod[:, lanes], axis=1, keepdims=True)
                        heads.append(jnp.sum(sc * vc[:, lanes], axis=0, keepdims=True))
                    out_rows.append(jnp.concatenate(heads, axis=1))
                oi_ref[pl.ds(base, SUBLANES), :] = jnp.concatenate(out_rows, axis=0)
                return carry

            lax.fori_loop(0, C // SUBLANES, rows_body, 0)

    gv = g_ref[...]
    og = ng_ref[...] * (gv * jax.nn.sigmoid(gv))
    for h in range(nhead):
        lanes = slice(h * HEAD_DIM, (h + 1) * HEAD_DIM)
        st = st_ref[h]
        for ci in range(nchunk):
            rows = slice(ci * C, (ci + 1) * C)
            cu = cum[rows, lanes]
            last = cu[C - 1: C, :]
            qe = (q[rows, lanes] * jnp.exp(cu)).astype(BF16)
            o = oi_ref[rows, lanes] + _nt_dot(qe, st.astype(BF16))
            ke = (k[rows, lanes] * jnp.exp(last - cu)).astype(BF16)
            st = st * jnp.exp(last) + _tn_dot(v_ref[rows, lanes].astype(BF16), ke)
            o = o * lax.rsqrt(jnp.mean(o * o, axis=-1, keepdims=True) + RMS_EPS)
            o_ref[rows, lanes] = (o * og[rows, lanes]).astype(o_ref.dtype)
        st_ref[h] = st


def _hgrn_core(proj, lb_logits, norm_g):
    S, D4 = proj.shape
    D = D4 // 4
    W = HG_HB * HEAD_DIM
    ng = D // W
    return pl.pallas_call(
        _hgrn_kernel,
        grid=(ng, S // HG_TB),
        in_specs=[
            pl.BlockSpec((HG_TB, W), lambda hg, t: (t, hg)),
            pl.BlockSpec((HG_TB, W), lambda hg, t: (t, hg + ng)),
            pl.BlockSpec((HG_TB, W), lambda hg, t: (t, hg + 2 * ng)),
            pl.BlockSpec((HG_TB, W), lambda hg, t: (t, hg + 3 * ng)),
            pl.BlockSpec((lb_logits.shape[0], W), lambda hg, t: (0, hg)),
            pl.BlockSpec((1, W), lambda hg, t: (0, hg)),
        ],
        out_specs=pl.BlockSpec((HG_TB, W), lambda hg, t: (t, hg)),
        out_shape=jax.ShapeDtypeStruct((S, D), BF16),
        scratch_shapes=[
            pltpu.VMEM((HG_HB, HEAD_DIM, HEAD_DIM), F32),
            pltpu.VMEM((HG_TB, W), F32),
            pltpu.VMEM((HG_TB, W), F32),
            pltpu.VMEM((HG_TB, W), F32),
            pltpu.VMEM((HG_TB, W), F32),
        ],
        compiler_params=_params("parallel", "arbitrary"),
        name="hgrn_core",
    )(proj, proj, proj, proj, lb_logits.astype(F32), norm_g.reshape(1, D))


def _moba_kernel(q_ref, k_ref, v_ref, o_ref, km_ref, sel_ref, s_ref):
    L = MB_BLOCK
    S = k_ref.shape[0]
    nb = S // L
    i = pl.program_id(1)
    c2 = HEAD_DIM ** -0.5 * LOG2_E

    @pl.when(i == 0)
    def _():
        def mean_body(n, carry):
            blk = k_ref[pl.ds(pl.multiple_of(n * L, L), L), :].astype(F32)
            km_ref[pl.ds(n, 1), :] = jnp.mean(blk, axis=0, keepdims=True)
            return carry
        lax.fori_loop(0, nb, mean_body, 0)

    q_raw = q_ref[...]
    q = (q_raw.astype(F32) * c2).astype(BF16)

    gate = _nt_dot(km_ref[...].astype(BF16), q_raw)
    nidx = lax.broadcasted_iota(jnp.int32, (nb, L), 0)
    gate = jnp.where(nidx < i, gate, -jnp.inf)
    sel = jnp.zeros((nb, L), F32)
    for _ in range(min(MB_TOPK, nb)):
        mx = jnp.max(gate, axis=0, keepdims=True)
        first = jnp.min(jnp.where(gate == mx, nidx, nb), axis=0, keepdims=True)
        pick = (nidx == first) & (mx > -jnp.inf)
        sel = jnp.where(pick, 1.0, sel)
        gate = jnp.where(pick, -jnp.inf, gate)
    sel_ref[...] = sel

    own = pl.ds(pl.multiple_of(i * L, L), L)
    st = _nt_dot(k_ref[own, :], q)
    kpos = lax.broadcasted_iota(jnp.int32, (L, L), 0)
    qpos = lax.broadcasted_iota(jnp.int32, (L, L), 1)
    st = jnp.where(kpos <= qpos, st, -jnp.inf)
    m0 = jnp.max(st, axis=0, keepdims=True)
    p = jnp.exp2(st - m0)
    l0 = jnp.sum(p, axis=0, keepdims=True)
    acc0 = _tn_dot(v_ref[own, :], p.astype(BF16))

    last_group = nb // MB_UNROLL - 1

    def block_rows(g, u):
        n = g * MB_UNROLL + u
        return n, pl.ds(pl.multiple_of(n * L, L), L)

    def score_group(g, slot):
        colmax = jnp.full((1, L), -jnp.inf, F32)
        for u in range(MB_UNROLL):
            n, r = block_rows(g, u)
            s = _nt_dot(k_ref[r, :], q)
            s_ref[slot, u] = s
            on = sel_ref[pl.ds(n, 1), :] > 0.0
            colmax = jnp.maximum(colmax, jnp.where(on, jnp.max(s, axis=0, keepdims=True), -jnp.inf))
        return colmax

    def attend_group(g, slot, colmax, m, l, acc):
        m_new = jnp.maximum(m, colmax)
        alpha = jnp.exp2(m - m_new)
        l = alpha * l
        acc = alpha * acc
        for u in range(MB_UNROLL):
            n, r = block_rows(g, u)
            on = sel_ref[pl.ds(n, 1), :] > 0.0
            p = jnp.exp2(s_ref[slot, u] - jnp.where(on, m_new, jnp.inf))
            l = l + jnp.sum(p, axis=0, keepdims=True)
            acc = acc + _tn_dot(v_ref[r, :], p.astype(BF16))
        return m_new, l, acc

    def pair_body(pi, carry):
        m, l, acc, cm_even = carry
        g = 2 * pi
        cm_odd = score_group(g + 1, 1)
        m, l, acc = attend_group(g, 0, cm_even, m, l, acc)
        cm_even = score_group(jnp.minimum(g + 2, last_group), 0)
        m, l, acc = attend_group(g + 1, 1, cm_odd, m, l, acc)
        return m, l, acc, cm_even

    ngroups = (i + MB_UNROLL - 1) // MB_UNROLL
    m, l, acc, _ = lax.fori_loop(0, (ngroups + 1) // 2, pair_body,
                                 (m0, l0, acc0, score_group(0, 0)))
    o_ref[...] = (acc / l).T.astype(o_ref.dtype)


def _moba_core(qkv):
    S, D3 = qkv.shape
    D = D3 // 3
    H = D // HEAD_DIM
    L = MB_BLOCK
    nb = S // L
    assert nb % (2 * MB_UNROLL) == 0
    return pl.pallas_call(
        _moba_kernel,
        grid=(H, nb),
        in_specs=[
            pl.BlockSpec((L, HEAD_DIM), lambda h, i: (i, h)),
            pl.BlockSpec((S, HEAD_DIM), lambda h, i: (0, H + h)),
            pl.BlockSpec((S, HEAD_DIM), lambda h, i: (0, 2 * H + h)),
        ],
        out_specs=pl.BlockSpec((L, HEAD_DIM), lambda h, i: (i, h)),
        out_shape=jax.ShapeDtypeStruct((S, D), BF16),
        scratch_shapes=[
            pltpu.VMEM((nb, HEAD_DIM), F32),
            pltpu.VMEM((nb, L), F32),
            pltpu.VMEM((2, MB_UNROLL, L, L), F32),
        ],
        compiler_params=_params("parallel", "arbitrary"),
        name="moba_core",
    )(qkv, qkv, qkv)


def kernel(x, lb_logits, l0_ffn1_in, l0_ffn1_out, l0_ln1_g, l0_ln1_b, l0_hg_in, l0_hg_norm_g, l0_hg_out, l0_ln2_g, l0_ln2_b, l0_ffn2_in, l0_ffn2_out, l0_ln3_g, l0_ln3_b, l1_ffn1_in, l1_ffn1_out, l1_ln1_g, l1_ln1_b, l1_mb_in, l1_mb_out, l1_ln2_g, l1_ln2_b, l1_ffn2_in, l1_ffn2_out, l1_ln3_g, l1_ln3_b):
    B, S, D = x.shape
    assert B == 1 and S % MB_BLOCK == 0 and S % FFN_TM == 0
    bf = lambda w: w.astype(BF16)
    h = x.reshape(S, D)

    h = _ffn_ln(h, bf(l0_ffn1_in), bf(l0_ffn1_out), l0_ln1_g, l0_ln1_b)
    proj = _proj(h, bf(l0_hg_in), F32)
    mix = _hgrn_core(proj, lb_logits, l0_hg_norm_g)
    h = _out_ln(h, mix, bf(l0_hg_out), l0_ln2_g, l0_ln2_b)
    h = _ffn_ln(h, bf(l0_ffn2_in), bf(l0_ffn2_out), l0_ln3_g, l0_ln3_b)

    h = _ffn_ln(h, bf(l1_ffn1_in), bf(l1_ffn1_out), l1_ln1_g, l1_ln1_b)
    qkv = _proj(h, bf(l1_mb_in), BF16)
    mix = _moba_core(qkv)
    h = _out_ln(h, mix, bf(l1_mb_out), l1_ln2_g, l1_ln2_b)
    h = _ffn_ln(h, bf(l1_ffn2_in), bf(l1_ffn2_out), l1_ln3_g, l1_ln3_b)
    return h.reshape(B, S, D)
```

```python
import functools

import jax
import jax.numpy as jnp
from jax import lax
from jax.experimental import pallas as pl
from jax.experimental.pallas import tpu as pltpu

F32 = jnp.float32
BF16 = jnp.bfloat16

DEPTH = 2
ALPHA = float((2 * DEPTH) ** 0.25)
LN_EPS = 1e-5
RMS_EPS = 1e-6
LOG2_E = 1.4426950408889634
HEAD_DIM = 128
SUBLANES = 8
BF16_SUBLANES = 16
LANES = 128
MB_BLOCK = 256
MB_TOPK = 3
MB_UNROLL = 4

VMEM_LIMIT_BYTES = 56 * 1024 * 1024

FFN_TM = 512
FFN_TF = 256
PROJ_TM = 1024
PROJ_TN = 512
OUT_TM = 512
OUT_TK = 1024
HG_TB = 256
HG_CHUNK = 64
HG_HB = 4
HG_SAFE_LOG = 60.0


def _params(*sem):
    return pltpu.CompilerParams(dimension_semantics=sem, vmem_limit_bytes=VMEM_LIMIT_BYTES)


LN_ROWS = 128
ACC_TN = 1024


def _layer_norm_inplace(o_ref, g_ref, b_ref):
    g = g_ref[...]
    b = b_ref[...]

    def body(r, carry):
        rows = pl.ds(pl.multiple_of(r * LN_ROWS, LN_ROWS), LN_ROWS)
        y = o_ref[rows, :]
        mu = jnp.mean(y, axis=-1, keepdims=True)
        yc = y - mu
        var = jnp.mean(yc * yc, axis=-1, keepdims=True)
        o_ref[rows, :] = yc * lax.rsqrt(var + LN_EPS) * g + b
        return carry

    lax.fori_loop(0, o_ref.shape[0] // LN_ROWS, body, 0)


def _accumulate_dot(o_ref, a, w_ref):
    tn = min(ACC_TN, o_ref.shape[1])
    assert o_ref.shape[1] % tn == 0
    for c in range(o_ref.shape[1] // tn):
        cols = slice(c * tn, (c + 1) * tn)
        o_ref[:, cols] += jnp.dot(a, w_ref[:, cols], preferred_element_type=F32)


def _ffn_ln_kernel(n_cast, x_ref, wg_ref, wu_ref, wo_ref, g_ref, b_ref, *refs):
    cast_in = refs[:n_cast]
    o_ref = refs[n_cast]
    cast_out = refs[n_cast + 1: 2 * n_cast + 1]
    xb_ref = refs[2 * n_cast + 1]
    j = pl.program_id(1)

    @pl.when(j == 0)
    def _():
        x = x_ref[...]
        xb_ref[...] = x.astype(BF16)
        o_ref[...] = ALPHA * x

    xb = xb_ref[...]
    gate = jnp.dot(xb, wg_ref[...], preferred_element_type=F32)
    up = jnp.dot(xb, wu_ref[...], preferred_element_type=F32)
    h = (0.5 * gate * jax.nn.sigmoid(gate) * up).astype(BF16)
    _accumulate_dot(o_ref, h, wo_ref)

    for src, dst in zip(cast_in, cast_out):
        dst[...] = src[...].astype(BF16)

    @pl.when(j == pl.num_programs(1) - 1)
    def _():
        _layer_norm_inplace(o_ref, g_ref, b_ref)


def _cast_tile_spec(w, ni, nj):
    R, C = w.shape
    rows = R // ni
    assert rows * ni == R and rows % BF16_SUBLANES == 0
    width = next(c for c in (LANES, 2 * LANES, 4 * LANES, 8 * LANES) if C % c == 0 and C // c <= nj)
    ncols = C // width
    return pl.BlockSpec((rows, width), lambda i, j: (i, jnp.minimum(j, ncols - 1)))


def _ffn_ln(x, w_in, w_out, g, b, cast_weights=()):
    S, D = x.shape
    F = w_out.shape[0]
    nf = F // FFN_TF
    ni = S // FFN_TM
    cast_specs = [_cast_tile_spec(w, ni, nf) for w in cast_weights]
    outs = pl.pallas_call(
        functools.partial(_ffn_ln_kernel, len(cast_weights)),
        grid=(ni, nf),
        in_specs=[
            pl.BlockSpec((FFN_TM, D), lambda i, j: (i, 0)),
            pl.BlockSpec((D, FFN_TF), lambda i, j: (0, j)),
            pl.BlockSpec((D, FFN_TF), lambda i, j: (0, j + nf)),
            pl.BlockSpec((FFN_TF, D), lambda i, j: (j, 0)),
            pl.BlockSpec((1, D), lambda i, j: (0, 0)),
            pl.BlockSpec((1, D), lambda i, j: (0, 0)),
        ] + cast_specs,
        out_specs=[pl.BlockSpec((FFN_TM, D), lambda i, j: (i, 0))] + cast_specs,
        out_shape=[jax.ShapeDtypeStruct((S, D), F32)]
                  + [jax.ShapeDtypeStruct(w.shape, BF16) for w in cast_weights],
        scratch_shapes=[pltpu.VMEM((FFN_TM, D), BF16)],
        compiler_params=_params("parallel", "arbitrary"),
        name="ffn_ln",
    )(x, w_in, w_in, w_out, g.reshape(1, D), b.reshape(1, D), *cast_weights)
    return outs[0], tuple(outs[1:])


def _proj_kernel(x_ref, w_ref, o_ref, xb_ref):
    @pl.when(pl.program_id(1) == 0)
    def _():
        xb_ref[...] = x_ref[...].astype(BF16)

    o_ref[...] = jnp.dot(xb_ref[...], w_ref[...], preferred_element_type=F32).astype(o_ref.dtype)


def _proj(x, w, out_dtype):
    S, D = x.shape
    N = w.shape[1]
    return pl.pallas_call(
        _proj_kernel,
        grid=(S // PROJ_TM, N // PROJ_TN),
        in_specs=[
            pl.BlockSpec((PROJ_TM, D), lambda i, j: (i, 0)),
            pl.BlockSpec((D, PROJ_TN), lambda i, j: (0, j)),
        ],
        out_specs=pl.BlockSpec((PROJ_TM, PROJ_TN), lambda i, j: (i, j)),
        out_shape=jax.ShapeDtypeStruct((S, N), out_dtype),
        scratch_shapes=[pltpu.VMEM((PROJ_TM, D), BF16)],
        compiler_params=_params("parallel", "arbitrary"),
        name="proj",
    )(x, w)


def _out_ln_kernel(x_ref, a_ref, w_ref, g_ref, b_ref, o_ref):
    k = pl.program_id(1)

    @pl.when(k == 0)
    def _():
        o_ref[...] = ALPHA * x_ref[...]

    _accumulate_dot(o_ref, a_ref[...], w_ref)

    @pl.when(k == pl.num_programs(1) - 1)
    def _():
        _layer_norm_inplace(o_ref, g_ref, b_ref)


def _out_ln(x, a, w, g, b):
    S, D = x.shape
    K = a.shape[1]
    return pl.pallas_call(
        _out_ln_kernel,
        grid=(S // OUT_TM, K // OUT_TK),
        in_specs=[
            pl.BlockSpec((OUT_TM, D), lambda i, k: (i, 0)),
            pl.BlockSpec((OUT_TM, OUT_TK), lambda i, k: (i, k)),
            pl.BlockSpec((OUT_TK, D), lambda i, k: (k, 0)),
            pl.BlockSpec((1, D), lambda i, k: (0, 0)),
            pl.BlockSpec((1, D), lambda i, k: (0, 0)),
        ],
        out_specs=pl.BlockSpec((OUT_TM, D), lambda i, k: (i, 0)),
        out_shape=jax.ShapeDtypeStruct((S, D), F32),
        compiler_params=_params("parallel", "arbitrary"),
        name="out_ln",
    )(x, a, w, g.reshape(1, D), b.reshape(1, D))


def _nt_dot(a, b):
    return lax.dot_general(a, b, (((1,), (1,)), ((), ())), preferred_element_type=F32)


def _tn_dot(a, b):
    return lax.dot_general(a, b, (((0,), (0,)), ((), ())), preferred_element_type=F32)


def _hgrn_kernel(q_ref, f_ref, v_ref, g_ref, lb_ref, ng_ref, o_ref,
                 st_ref, cum_ref, qs_ref, oi_ref):
    TB, W = q_ref.shape
    C = HG_CHUNK
    nchunk = TB // C
    nhead = W // HEAD_DIM

    @pl.when(pl.program_id(1) == 0)
    def _():
        st_ref[...] = jnp.zeros_like(st_ref)

    lg = lb_ref[...]
    ex = jnp.exp(lg - jnp.max(lg, axis=0, keepdims=True))
    lb = ex[0:1, :] / jnp.sum(ex, axis=0, keepdims=True)
    qv = q_ref[...]
    q = qv * jax.nn.sigmoid(qv)
    forget = lb + (1.0 - lb) * jax.nn.sigmoid(f_ref[...])
    k = 1.0 - forget
    logf = jnp.log(forget)

    r = lax.broadcasted_iota(jnp.int32, (TB, TB), 0)
    c = lax.broadcasted_iota(jnp.int32, (TB, TB), 1)
    tri = jnp.where((c <= r) & (c // C == r // C), 1.0, 0.0).astype(BF16)
    hi = logf.astype(BF16)
    r1 = logf - hi.astype(F32)
    mid = r1.astype(BF16)
    lo = (r1 - mid.astype(F32)).astype(BF16)
    cum = (jnp.dot(tri, hi, preferred_element_type=F32)
           + jnp.dot(tri, mid, preferred_element_type=F32)
           + jnp.dot(tri, lo, preferred_element_type=F32))

    bound = jnp.zeros((1, W), F32)
    for ci in range(nchunk):
        mrow = cum[ci * C + C // 2 - 1: ci * C + C // 2, :]
        lrow = cum[ci * C + C - 1: ci * C + C, :]
        bound = jnp.maximum(bound, jnp.maximum(-mrow, mrow - lrow))
    safe = jnp.max(bound) <= HG_SAFE_LOG

    tr = lax.broadcasted_iota(jnp.int32, (C, C), 0)
    ts = lax.broadcasted_iota(jnp.int32, (C, C), 1)
    causal = ts <= tr

    gv = g_ref[...]
    og = ng_ref[...] * (gv * jax.nn.sigmoid(gv))

    def factorised_intra(rows, lanes):
        cu = cum[rows, lanes]
        m = cu[C // 2 - 1: C // 2, :]
        qd = (q[rows, lanes] * jnp.exp(cu - m)).astype(BF16)
        kd = (k[rows, lanes] * jnp.exp(m - cu)).astype(BF16)
        a = jnp.where(causal, _nt_dot(qd, kd), 0.0).astype(BF16)
        return jnp.dot(a, v_ref[rows, lanes].astype(BF16), preferred_element_type=F32)

    def stored_intra(rows, lanes):
        return oi_ref[rows, lanes]

    def recurrence(intra):
        for h in range(nhead):
            lanes = slice(h * HEAD_DIM, (h + 1) * HEAD_DIM)
            st = st_ref[h]
            for ci in range(nchunk):
                rows = slice(ci * C, (ci + 1) * C)
                cu = cum[rows, lanes]
                last = cu[C - 1: C, :]
                qe = (q[rows, lanes] * jnp.exp(cu)).astype(BF16)
                o = intra(rows, lanes) + _nt_dot(qe, st.astype(BF16))
                ke = (k[rows, lanes] * jnp.exp(last - cu)).astype(BF16)
                st = st * jnp.exp(last) + _tn_dot(v_ref[rows, lanes].astype(BF16), ke)
                o = o * lax.rsqrt(jnp.mean(o * o, axis=-1, keepdims=True) + RMS_EPS)
                o_ref[rows, lanes] = (o * og[rows, lanes]).astype(o_ref.dtype)
            st_ref[h] = st

    @pl.when(safe)
    def _():
        recurrence(factorised_intra)

    @pl.when(jnp.logical_not(safe))
    def _():
        cum_ref[...] = cum
        qs_ref[...] = q
        sidx = lax.broadcasted_iota(jnp.int32, (C, W), 0)
        for ci in range(nchunk):
            rows = slice(ci * C, (ci + 1) * C)
            cu = cum[rows, :]
            kc = k[rows, :]
            vc = v_ref[rows, :]

            def rows_body(t8, carry, ci=ci, cu=cu, kc=kc, vc=vc):
                base = pl.multiple_of(ci * C + t8 * SUBLANES, SUBLANES)
                c8 = cum_ref[pl.ds(base, SUBLANES), :]
                q8 = qs_ref[pl.ds(base, SUBLANES), :]
                out_rows = []
                for r in range(SUBLANES):
                    t = t8 * SUBLANES + r
                    dec = jnp.exp(jnp.where(sidx <= t, c8[r:r + 1, :] - cu, -jnp.inf))
                    prod = q8[r:r + 1, :] * dec * kc
                    heads = []
                    for h in range(nhead):
                        lanes = slice(h * HEAD_DIM, (h + 1) * HEAD_DIM)
                        sc = jnp.sum(prod[:, lanes], axis=1, keepdims=True)
                        heads.append(jnp.sum(sc * vc[:, lanes], axis=0, keepdims=True))
                    out_rows.append(jnp.concatenate(heads, axis=1))
                oi_ref[pl.ds(base, SUBLANES), :] = jnp.concatenate(out_rows, axis=0)
                return carry

            lax.fori_loop(0, C // SUBLANES, rows_body, 0)
        recurrence(stored_intra)


def _hgrn_core(proj, lb_logits, norm_g):
    S, D4 = proj.shape
    D = D4 // 4
    W = HG_HB * HEAD_DIM
    ng = D // W
    return pl.pallas_call(
        _hgrn_kernel,
        grid=(ng, S // HG_TB),
        in_specs=[
            pl.BlockSpec((HG_TB, W), lambda hg, t: (t, hg)),
            pl.BlockSpec((HG_TB, W), lambda hg, t: (t, hg + ng)),
            pl.BlockSpec((HG_TB, W), lambda hg, t: (t, hg + 2 * ng)),
            pl.BlockSpec((HG_TB, W), lambda hg, t: (t, hg + 3 * ng)),
            pl.BlockSpec((lb_logits.shape[0], W), lambda hg, t: (0, hg)),
            pl.BlockSpec((1, W), lambda hg, t: (0, hg)),
        ],
        out_specs=pl.BlockSpec((HG_TB, W), lambda hg, t: (t, hg)),
        out_shape=jax.ShapeDtypeStruct((S, D), BF16),
        scratch_shapes=[
            pltpu.VMEM((HG_HB, HEAD_DIM, HEAD_DIM), F32),
            pltpu.VMEM((HG_TB, W), F32),
            pltpu.VMEM((HG_TB, W), F32),
            pltpu.VMEM((HG_TB, W), F32),
        ],
        compiler_params=_params("parallel", "arbitrary"),
        name="hgrn_core",
    )(proj, proj, proj, proj, lb_logits.astype(F32), norm_g.reshape(1, D))


def _moba_kernel(q_ref, k_ref, v_ref, o_ref, km_ref, sel_ref, s_ref):
    L = MB_BLOCK
    S = k_ref.shape[0]
    nb = S // L
    i = pl.program_id(1)
    c2 = HEAD_DIM ** -0.5 * LOG2_E

    @pl.when(i == 0)
    def _():
        def mean_body(n, carry):
            blk = k_ref[pl.ds(pl.multiple_of(n * L, L), L), :].astype(F32)
            km_ref[pl.ds(n, 1), :] = jnp.mean(blk, axis=0, keepdims=True)
            return carry
        lax.fori_loop(0, nb, mean_body, 0)

    q_raw = q_ref[...]
    q = (q_raw.astype(F32) * c2).astype(BF16)

    gate = _nt_dot(km_ref[...].astype(BF16), q_raw)
    nidx = lax.broadcasted_iota(jnp.int32, (nb, L), 0)
    gate = jnp.where(nidx < i, gate, -jnp.inf)
    sel = jnp.zeros((nb, L), F32)
    for _ in range(min(MB_TOPK, nb)):
        mx = jnp.max(gate, axis=0, keepdims=True)
        first = jnp.min(jnp.where(gate == mx, nidx, nb), axis=0, keepdims=True)
        pick = (nidx == first) & (mx > -jnp.inf)
        sel = jnp.where(pick, 1.0, sel)
        gate = jnp.where(pick, -jnp.inf, gate)
    sel_ref[...] = sel

    own = pl.ds(pl.multiple_of(i * L, L), L)
    st = _nt_dot(k_ref[own, :], q)
    kpos = lax.broadcasted_iota(jnp.int32, (L, L), 0)
    qpos = lax.broadcasted_iota(jnp.int32, (L, L), 1)
    st = jnp.where(kpos <= qpos, st, -jnp.inf)
    m0 = jnp.max(st, axis=0, keepdims=True)
    p = jnp.exp2(st - m0)
    l0 = jnp.sum(p, axis=0, keepdims=True)
    acc0 = _tn_dot(v_ref[own, :], p.astype(BF16))

    last_group = nb // MB_UNROLL - 1

    def block_rows(g, u):
        n = g * MB_UNROLL + u
        return n, pl.ds(pl.multiple_of(n * L, L), L)

    def score_group(g, slot):
        colmax = jnp.full((1, L), -jnp.inf, F32)
        for u in range(MB_UNROLL):
            n, r = block_rows(g, u)
            s = _nt_dot(k_ref[r, :], q)
            s_ref[slot, u] = s
            on = sel_ref[pl.ds(n, 1), :] > 0.0
            colmax = jnp.maximum(colmax, jnp.where(on, jnp.max(s, axis=0, keepdims=True), -jnp.inf))
        return colmax

    def attend_group(g, slot, colmax, m, l, acc):
        m_new = jnp.maximum(m, colmax)
        alpha = jnp.exp2(m - m_new)
        l = alpha * l
        acc = alpha * acc
        for u in range(MB_UNROLL):
            n, r = block_rows(g, u)
            on = sel_ref[pl.ds(n, 1), :] > 0.0
            p = jnp.exp2(s_ref[slot, u] - jnp.where(on, m_new, jnp.inf))
            l = l + jnp.sum(p, axis=0, keepdims=True)
            acc = acc + _tn_dot(v_ref[r, :], p.astype(BF16))
        return m_new, l, acc

    def pair_body(pi, carry):
        m, l, acc, cm_even = carry
        g = 2 * pi
        cm_odd = score_group(g + 1, 1)
        m, l, acc = attend_group(g, 0, cm_even, m, l, acc)
        cm_even = score_group(jnp.minimum(g + 2, last_group), 0)
        m, l, acc = attend_group(g + 1, 1, cm_odd, m, l, acc)
        return m, l, acc, cm_even

    ngroups = (i + MB_UNROLL - 1) // MB_UNROLL
    m, l, acc, _ = lax.fori_loop(0, (ngroups + 1) // 2, pair_body,
                                 (m0, l0, acc0, score_group(0, 0)))
    o_ref[...] = (acc / l).T.astype(o_ref.dtype)


def _moba_core(qkv):
    S, D3 = qkv.shape
    D = D3 // 3
    H = D // HEAD_DIM
    L = MB_BLOCK
    nb = S // L
    assert nb % (2 * MB_UNROLL) == 0
    return pl.pallas_call(
        _moba_kernel,
        grid=(H, nb),
        in_specs=[
            pl.BlockSpec((L, HEAD_DIM), lambda h, i: (i, h)),
            pl.BlockSpec((S, HEAD_DIM), lambda h, i: (0, H + h)),
            pl.BlockSpec((S, HEAD_DIM), lambda h, i: (0, 2 * H + h)),
        ],
        out_specs=pl.BlockSpec((L, HEAD_DIM), lambda h, i: (i, h)),
        out_shape=jax.ShapeDtypeStruct((S, D), BF16),
        scratch_shapes=[
            pltpu.VMEM((nb, HEAD_DIM), F32),
            pltpu.VMEM((nb, L), F32),
            pltpu.VMEM((2, MB_UNROLL, L, L), F32),
        ],
        compiler_params=_params("parallel", "arbitrary"),
        name="moba_core",
    )(qkv, qkv, qkv)


def kernel(x, lb_logits, l0_ffn1_in, l0_ffn1_out, l0_ln1_g, l0_ln1_b, l0_hg_in, l0_hg_norm_g, l0_hg_out, l0_ln2_g, l0_ln2_b, l0_ffn2_in, l0_ffn2_out, l0_ln3_g, l0_ln3_b, l1_ffn1_in, l1_ffn1_out, l1_ln1_g, l1_ln1_b, l1_mb_in, l1_mb_out, l1_ln2_g, l1_ln2_b, l1_ffn2_in, l1_ffn2_out, l1_ln3_g, l1_ln3_b):
    B, S, D = x.shape
    assert B == 1 and S % MB_BLOCK == 0 and S % FFN_TM == 0
    bf = lambda w: w.astype(BF16)
    h = x.reshape(S, D)

    h, (hg_in, hg_out, f2_in, f2_out) = _ffn_ln(
        h, bf(l0_ffn1_in), bf(l0_ffn1_out), l0_ln1_g, l0_ln1_b,
        cast_weights=(l0_hg_in, l0_hg_out, l0_ffn2_in, l0_ffn2_out))
    proj = _proj(h, hg_in, F32)
    mix = _hgrn_core(proj, lb_logits, l0_hg_norm_g)
    h = _out_ln(h, mix, hg_out, l0_ln2_g, l0_ln2_b)
    h, (f1_in, f1_out) = _ffn_ln(h, f2_in, f2_out, l0_ln3_g, l0_ln3_b,
                                 cast_weights=(l1_ffn1_in, l1_ffn1_out))

    h, (mb_in, mb_out, f2_in, f2_out) = _ffn_ln(
        h, f1_in, f1_out, l1_ln1_g, l1_ln1_b,
        cast_weights=(l1_mb_in, l1_mb_out, l1_ffn2_in, l1_ffn2_out))
    qkv = _proj(h, mb_in, BF16)
    mix = _moba_core(qkv)
    h = _out_ln(h, mix, mb_out, l1_ln2_g, l1_ln2_b)
    h, _ = _ffn_ln(h, f2_in, f2_out, l1_ln3_g, l1_ln3_b)
    return h.reshape(B, S, D)
```

```python
import functools

import jax
import jax.numpy as jnp
from jax import lax
from jax.experimental import pallas as pl
from jax.experimental.pallas import tpu as pltpu

F32 = jnp.float32
BF16 = jnp.bfloat16

DEPTH = 2
ALPHA = float((2 * DEPTH) ** 0.25)
LN_EPS = 1e-5
RMS_EPS = 1e-6
LOG2_E = 1.4426950408889634
HEAD_DIM = 128
SUBLANES = 8
BF16_SUBLANES = 16
LANES = 128
MB_BLOCK = 256
MB_TOPK = 3
MB_UNROLL = 4

VMEM_LIMIT_BYTES = 56 * 1024 * 1024

FFN_TM = 512
FFN_TF = 256
PROJ_TM = 1024
PROJ_TN = 512
OUT_TM = 512
OUT_TK = 1024
HG_TB = 256
HG_CHUNK = 64
HG_HB = 4
HG_SAFE_LOG = 60.0


def _params(*sem):
    return pltpu.CompilerParams(dimension_semantics=sem, vmem_limit_bytes=VMEM_LIMIT_BYTES)


LN_ROWS = 128
ACC_TN = 1024


def _layer_norm_inplace(o_ref, g_ref, b_ref):
    g = g_ref[...]
    b = b_ref[...]

    def body(r, carry):
        rows = pl.ds(pl.multiple_of(r * LN_ROWS, LN_ROWS), LN_ROWS)
        y = o_ref[rows, :]
        mu = jnp.mean(y, axis=-1, keepdims=True)
        yc = y - mu
        var = jnp.mean(yc * yc, axis=-1, keepdims=True)
        o_ref[rows, :] = yc * lax.rsqrt(var + LN_EPS) * g + b
        return carry

    lax.fori_loop(0, o_ref.shape[0] // LN_ROWS, body, 0)


def _accumulate_dot(o_ref, a, w_ref):
    tn = min(ACC_TN, o_ref.shape[1])
    assert o_ref.shape[1] % tn == 0
    for c in range(o_ref.shape[1] // tn):
        cols = slice(c * tn, (c + 1) * tn)
        o_ref[:, cols] += jnp.dot(a, w_ref[:, cols], preferred_element_type=F32)


def _ffn_ln_kernel(n_cast, x_ref, wg_ref, wu_ref, wo_ref, g_ref, b_ref, *refs):
    cast_in = refs[:n_cast]
    o_ref = refs[n_cast]
    cast_out = refs[n_cast + 1: 2 * n_cast + 1]
    xb_ref = refs[2 * n_cast + 1]
    j = pl.program_id(1)

    @pl.when(j == 0)
    def _():
        x = x_ref[...]
        xb_ref[...] = x.astype(BF16)
        o_ref[...] = ALPHA * x

    xb = xb_ref[...]
    gate = jnp.dot(xb, wg_ref[...], preferred_element_type=F32)
    up = jnp.dot(xb, wu_ref[...], preferred_element_type=F32)
    h = (0.5 * gate * jax.nn.sigmoid(gate) * up).astype(BF16)
    _accumulate_dot(o_ref, h, wo_ref)

    for src, dst in zip(cast_in, cast_out):
        dst[...] = src[...].astype(BF16)

    @pl.when(j == pl.num_programs(1) - 1)
    def _():
        _layer_norm_inplace(o_ref, g_ref, b_ref)


def _cast_tile_spec(w, ni, nj):
    R, C = w.shape
    rows = R // ni
    assert rows * ni == R and rows % BF16_SUBLANES == 0
    width = next(c for c in (LANES, 2 * LANES, 4 * LANES, 8 * LANES) if C % c == 0 and C // c <= nj)
    ncols = C // width
    return pl.BlockSpec((rows, width), lambda i, j: (i, jnp.minimum(j, ncols - 1)))


def _ffn_ln(x, w_in, w_out, g, b, cast_weights=()):
    S, D = x.shape
    F = w_out.shape[0]
    nf = F // FFN_TF
    ni = S // FFN_TM
    cast_specs = [_cast_tile_spec(w, ni, nf) for w in cast_weights]
    outs = pl.pallas_call(
        functools.partial(_ffn_ln_kernel, len(cast_weights)),
        grid=(ni, nf),
        in_specs=[
            pl.BlockSpec((FFN_TM, D), lambda i, j: (i, 0)),
            pl.BlockSpec((D, FFN_TF), lambda i, j: (0, j)),
            pl.BlockSpec((D, FFN_TF), lambda i, j: (0, j + nf)),
            pl.BlockSpec((FFN_TF, D), lambda i, j: (j, 0)),
            pl.BlockSpec((1, D), lambda i, j: (0, 0)),
            pl.BlockSpec((1, D), lambda i, j: (0, 0)),
        ] + cast_specs,
        out_specs=[pl.BlockSpec((FFN_TM, D), lambda i, j: (i, 0))] + cast_specs,
        out_shape=[jax.ShapeDtypeStruct((S, D), F32)]
                  + [jax.ShapeDtypeStruct(w.shape, BF16) for w in cast_weights],
        scratch_shapes=[pltpu.VMEM((FFN_TM, D), BF16)],
        compiler_params=_params("parallel", "arbitrary"),
        name="ffn_ln",
    )(x, w_in, w_in, w_out, g.reshape(1, D), b.reshape(1, D), *cast_weights)
    return outs[0], tuple(outs[1:])


def _proj_kernel(x_ref, w_ref, o_ref, xb_ref):
    @pl.when(pl.program_id(1) == 0)
    def _():
        xb_ref[...] = x_ref[...].astype(BF16)

    o_ref[...] = jnp.dot(xb_ref[...], w_ref[...], preferred_element_type=F32).astype(o_ref.dtype)


def _proj(x, w, out_dtype):
    S, D = x.shape
    N = w.shape[1]
    return pl.pallas_call(
        _proj_kernel,
        grid=(S // PROJ_TM, N // PROJ_TN),
        in_specs=[
            pl.BlockSpec((PROJ_TM, D), lambda i, j: (i, 0)),
            pl.BlockSpec((D, PROJ_TN), lambda i, j: (0, j)),
        ],
        out_specs=pl.BlockSpec((PROJ_TM, PROJ_TN), lambda i, j: (i, j)),
        out_shape=jax.ShapeDtypeStruct((S, N), out_dtype),
        scratch_shapes=[pltpu.VMEM((PROJ_TM, D), BF16)],
        compiler_params=_params("parallel", "arbitrary"),
        name="proj",
    )(x, w)


def _out_ln_kernel(x_ref, a_ref, w_ref, g_ref, b_ref, o_ref):
    k = pl.program_id(1)

    @pl.when(k == 0)
    def _():
        o_ref[...] = ALPHA * x_ref[...]

    _accumulate_dot(o_ref, a_ref[...], w_ref)

    @pl.when(k == pl.num_programs(1) - 1)
    def _():
        _layer_norm_inplace(o_ref, g_ref, b_ref)


def _out_ln(x, a, w, g, b):
    S, D = x.shape
    K = a.shape[1]
    return pl.pallas_call(
        _out_ln_kernel,
        grid=(S // OUT_TM, K // OUT_TK),
        in_specs=[
            pl.BlockSpec((OUT_TM, D), lambda i, k: (i, 0)),
            pl.BlockSpec((OUT_TM, OUT_TK), lambda i, k: (i, k)),
            pl.BlockSpec((OUT_TK, D), lambda i, k: (k, 0)),
            pl.BlockSpec((1, D), lambda i, k: (0, 0)),
            pl.BlockSpec((1, D), lambda i, k: (0, 0)),
        ],
        out_specs=pl.BlockSpec((OUT_TM, D), lambda i, k: (i, 0)),
        out_shape=jax.ShapeDtypeStruct((S, D), F32),
        compiler_params=_params("parallel", "arbitrary"),
        name="out_ln",
    )(x, a, w, g.reshape(1, D), b.reshape(1, D))


def _nt_dot(a, b):
    return lax.dot_general(a, b, (((1,), (1,)), ((), ())), preferred_element_type=F32)


def _tn_dot(a, b):
    return lax.dot_general(a, b, (((0,), (0,)), ((), ())), preferred_element_type=F32)


def _hgrn_kernel(q_ref, f_ref, v_ref, g_ref, lb_ref, ng_ref, o_ref,
                 st_ref, cum_ref, qs_ref, oi_ref):
    TB, W = q_ref.shape
    C = HG_CHUNK
    nchunk = TB // C
    nhead = W // HEAD_DIM

    @pl.when(pl.program_id(1) == 0)
    def _():
        st_ref[...] = jnp.zeros_like(st_ref)

    lg = lb_ref[...]
    ex = jnp.exp(lg - jnp.max(lg, axis=0, keepdims=True))
    lb = ex[0:1, :] / jnp.sum(ex, axis=0, keepdims=True)
    qv = q_ref[...]
    q = qv * jax.nn.sigmoid(qv)
    forget = lb + (1.0 - lb) * jax.nn.sigmoid(f_ref[...])
    k = 1.0 - forget
    logf = jnp.log(forget)

    r = lax.broadcasted_iota(jnp.int32, (TB, TB), 0)
    c = lax.broadcasted_iota(jnp.int32, (TB, TB), 1)
    tri = jnp.where((c <= r) & (c // C == r // C), 1.0, 0.0).astype(BF16)
    hi = logf.astype(BF16)
    r1 = logf - hi.astype(F32)
    mid = r1.astype(BF16)
    lo = (r1 - mid.astype(F32)).astype(BF16)
    cum = (jnp.dot(tri, hi, preferred_element_type=F32)
           + jnp.dot(tri, mid, preferred_element_type=F32)
           + jnp.dot(tri, lo, preferred_element_type=F32))

    bound = jnp.zeros((1, W), F32)
    for ci in range(nchunk):
        mrow = cum[ci * C + C // 2 - 1: ci * C + C // 2, :]
        lrow = cum[ci * C + C - 1: ci * C + C, :]
        bound = jnp.maximum(bound, jnp.maximum(-mrow, mrow - lrow))
    safe = jnp.max(bound) <= HG_SAFE_LOG

    tr = lax.broadcasted_iota(jnp.int32, (C, C), 0)
    ts = lax.broadcasted_iota(jnp.int32, (C, C), 1)
    causal = ts <= tr

    gv = g_ref[...]
    og = ng_ref[...] * (gv * jax.nn.sigmoid(gv))

    def factorised_intra(rows, lanes):
        cu = cum[rows, lanes]
        m = cu[C // 2 - 1: C // 2, :]
        qd = (q[rows, lanes] * jnp.exp(cu - m)).astype(BF16)
        kd = (k[rows, lanes] * jnp.exp(m - cu)).astype(BF16)
        a = jnp.where(causal, _nt_dot(qd, kd), 0.0).astype(BF16)
        return jnp.dot(a, v_ref[rows, lanes].astype(BF16), preferred_element_type=F32)

    def stored_intra(rows, lanes):
        return oi_ref[rows, lanes]

    def recurrence(intra):
        for h in range(nhead):
            lanes = slice(h * HEAD_DIM, (h + 1) * HEAD_DIM)
            st = st_ref[h]
            for ci in range(nchunk):
                rows = slice(ci * C, (ci + 1) * C)
                cu = cum[rows, lanes]
                last = cu[C - 1: C, :]
                qe = (q[rows, lanes] * jnp.exp(cu)).astype(BF16)
                o = intra(rows, lanes) + _nt_dot(qe, st.astype(BF16))
                ke = (k[rows, lanes] * jnp.exp(last - cu)).astype(BF16)
                st = st * jnp.exp(last) + _tn_dot(v_ref[rows, lanes].astype(BF16), ke)
                o = o * lax.rsqrt(jnp.mean(o * o, axis=-1, keepdims=True) + RMS_EPS)
                o_ref[rows, lanes] = (o * og[rows, lanes]).astype(o_ref.dtype)
            st_ref[h] = st

    @pl.when(safe)
    def _():
        recurrence(factorised_intra)

    @pl.when(jnp.logical_not(safe))
    def _():
        cum_ref[...] = cum
        qs_ref[...] = q
        sidx = lax.broadcasted_iota(jnp.int32, (C, W), 0)
        for ci in range(nchunk):
            rows = slice(ci * C, (ci + 1) * C)
            cu = cum[rows, :]
            kc = k[rows, :]
            vc = v_ref[rows, :]

            def rows_body(t8, carry, ci=ci, cu=cu, kc=kc, vc=vc):
                base = pl.multiple_of(ci * C + t8 * SUBLANES, SUBLANES)
                c8 = cum_ref[pl.ds(base, SUBLANES), :]
                q8 = qs_ref[pl.ds(base, SUBLANES), :]
                out_rows = []
                for r in range(SUBLANES):
                    t = t8 * SUBLANES + r
                    dec = jnp.exp(jnp.where(sidx <= t, c8[r:r + 1, :] - cu, -jnp.inf))
                    prod = q8[r:r + 1, :] * dec * kc
                    heads = []
                    for h in range(nhead):
                        lanes = slice(h * HEAD_DIM, (h + 1) * HEAD_DIM)
                        sc = jnp.sum(prod[:, lanes], axis=1, keepdims=True)
                        heads.append(jnp.sum(sc * vc[:, lanes], axis=0, keepdims=True))
                    out_rows.append(jnp.concatenate(heads, axis=1))
                oi_ref[pl.ds(base, SUBLANES), :] = jnp.concatenate(out_rows, axis=0)
                return carry

            lax.fori_loop(0, C // SUBLANES, rows_body, 0)
        recurrence(stored_intra)


def _hgrn_core(proj, lb_logits, norm_g):
    S, D4 = proj.shape
    D = D4 // 4
    W = HG_HB * HEAD_DIM
    ng = D // W
    return pl.pallas_call(
        _hgrn_kernel,
        grid=(ng, S // HG_TB),
        in_specs=[
            pl.BlockSpec((HG_TB, W), lambda hg, t: (t, hg)),
            pl.BlockSpec((HG_TB, W), lambda hg, t: (t, hg + ng)),
            pl.BlockSpec((HG_TB, W), lambda hg, t: (t, hg + 2 * ng)),
            pl.BlockSpec((HG_TB, W), lambda hg, t: (t, hg + 3 * ng)),
            pl.BlockSpec((lb_logits.shape[0], W), lambda hg, t: (0, hg)),
            pl.BlockSpec((1, W), lambda hg, t: (0, hg)),
        ],
        out_specs=pl.BlockSpec((HG_TB, W), lambda hg, t: (t, hg)),
        out_shape=jax.ShapeDtypeStruct((S, D), BF16),
        scratch_shapes=[
            pltpu.VMEM((HG_HB, HEAD_DIM, HEAD_DIM), F32),
            pltpu.VMEM((HG_TB, W), F32),
            pltpu.VMEM((HG_TB, W), F32),
            pltpu.VMEM((HG_TB, W), F32),
        ],
        compiler_params=_params("parallel", "arbitrary"),
        name="hgrn_core",
    )(proj, proj, proj, proj, lb_logits.astype(F32), norm_g.reshape(1, D))


def _moba_kernel(q_ref, k_ref, v_ref, o_ref, km_ref, sel_ref, s_ref):
    L = MB_BLOCK
    S = k_ref.shape[0]
    nb = S // L
    c2 = HEAD_DIM ** -0.5 * LOG2_E
    last_group = nb // MB_UNROLL - 1

    def mean_body(n, carry):
        blk = k_ref[pl.ds(pl.multiple_of(n * L, L), L), :].astype(F32)
        km_ref[pl.ds(n, 1), :] = jnp.mean(blk, axis=0, keepdims=True)
        return carry
    lax.fori_loop(0, nb, mean_body, 0)

    def tile_body(i, carry):
        _moba_tile(i, q_ref, k_ref, v_ref, o_ref, km_ref, sel_ref, s_ref, c2, last_group)
        return carry
    lax.fori_loop(0, nb, tile_body, 0)


def _moba_tile(i, q_ref, k_ref, v_ref, o_ref, km_ref, sel_ref, s_ref, c2, last_group):
    L = MB_BLOCK
    nb = k_ref.shape[0] // L
    own = pl.ds(pl.multiple_of(i * L, L), L)
    q_raw = q_ref[own, :]
    q = (q_raw.astype(F32) * c2).astype(BF16)

    gate = _nt_dot(km_ref[...].astype(BF16), q_raw)
    nidx = lax.broadcasted_iota(jnp.int32, (nb, L), 0)
    gate = jnp.where(nidx < i, gate, -jnp.inf)
    sel = jnp.zeros((nb, L), F32)
    for _ in range(min(MB_TOPK, nb)):
        mx = jnp.max(gate, axis=0, keepdims=True)
        first = jnp.min(jnp.where(gate == mx, nidx, nb), axis=0, keepdims=True)
        pick = (nidx == first) & (mx > -jnp.inf)
        sel = jnp.where(pick, 1.0, sel)
        gate = jnp.where(pick, -jnp.inf, gate)
    sel_ref[...] = sel

    st = _nt_dot(k_ref[own, :], q)
    kpos = lax.broadcasted_iota(jnp.int32, (L, L), 0)
    qpos = lax.broadcasted_iota(jnp.int32, (L, L), 1)
    st = jnp.where(kpos <= qpos, st, -jnp.inf)
    m0 = jnp.max(st, axis=0, keepdims=True)
    p = jnp.exp2(st - m0)
    l0 = jnp.sum(p, axis=0, keepdims=True)
    acc0 = _tn_dot(v_ref[own, :], p.astype(BF16))

    def block_rows(g, u):
        n = g * MB_UNROLL + u
        return n, pl.ds(pl.multiple_of(n * L, L), L)

    def score_group(g, slot):
        colmax = jnp.full((1, L), -jnp.inf, F32)
        for u in range(MB_UNROLL):
            n, r = block_rows(g, u)
            s = _nt_dot(k_ref[r, :], q)
            s_ref[slot, u] = s
            on = sel_ref[pl.ds(n, 1), :] > 0.0
            colmax = jnp.maximum(colmax, jnp.where(on, jnp.max(s, axis=0, keepdims=True), -jnp.inf))
        return colmax

    def attend_group(g, slot, colmax, m, l, acc):
        m_new = jnp.maximum(m, colmax)
        alpha = jnp.exp2(m - m_new)
        l = alpha * l
        acc = alpha * acc
        for u in range(MB_UNROLL):
            n, r = block_rows(g, u)
            on = sel_ref[pl.ds(n, 1), :] > 0.0
            p = jnp.exp2(s_ref[slot, u] - jnp.where(on, m_new, jnp.inf))
            l = l + jnp.sum(p, axis=0, keepdims=True)
            acc = acc + _tn_dot(v_ref[r, :], p.astype(BF16))
        return m_new, l, acc

    def pair_body(pi, carry):
        m, l, acc, cm_even = carry
        g = 2 * pi
        cm_odd = score_group(g + 1, 1)
        m, l, acc = attend_group(g, 0, cm_even, m, l, acc)
        cm_even = score_group(jnp.minimum(g + 2, last_group), 0)
        m, l, acc = attend_group(g + 1, 1, cm_odd, m, l, acc)
        return m, l, acc, cm_even

    ngroups = (i + MB_UNROLL - 1) // MB_UNROLL
    m, l, acc, cm_even = lax.fori_loop(0, ngroups // 2, pair_body,
                                       (m0, l0, acc0, score_group(0, 0)))
    m, l, acc = lax.cond(ngroups % 2 == 1,
                         lambda: attend_group(ngroups - 1, 0, cm_even, m, l, acc),
                         lambda: (m, l, acc))
    o_ref[own, :] = (acc / l).T.astype(o_ref.dtype)


def _moba_core(qkv):
    S, D3 = qkv.shape
    D = D3 // 3
    H = D // HEAD_DIM
    L = MB_BLOCK
    nb = S // L
    assert nb % (2 * MB_UNROLL) == 0
    return pl.pallas_call(
        _moba_kernel,
        grid=(H,),
        in_specs=[
            pl.BlockSpec((S, HEAD_DIM), lambda h: (0, h)),
            pl.BlockSpec((S, HEAD_DIM), lambda h: (0, H + h)),
            pl.BlockSpec((S, HEAD_DIM), lambda h: (0, 2 * H + h)),
        ],
        out_specs=pl.BlockSpec((S, HEAD_DIM), lambda h: (0, h)),
        out_shape=jax.ShapeDtypeStruct((S, D), BF16),
        scratch_shapes=[
            pltpu.VMEM((nb, HEAD_DIM), F32),
            pltpu.VMEM((nb, L), F32),
            pltpu.VMEM((2, MB_UNROLL, L, L), F32),
        ],
        compiler_params=_params("parallel"),
        name="moba_core",
    )(qkv, qkv, qkv)


def kernel(x, lb_logits, l0_ffn1_in, l0_ffn1_out, l0_ln1_g, l0_ln1_b, l0_hg_in, l0_hg_norm_g, l0_hg_out, l0_ln2_g, l0_ln2_b, l0_ffn2_in, l0_ffn2_out, l0_ln3_g, l0_ln3_b, l1_ffn1_in, l1_ffn1_out, l1_ln1_g, l1_ln1_b, l1_mb_in, l1_mb_out, l1_ln2_g, l1_ln2_b, l1_ffn2_in, l1_ffn2_out, l1_ln3_g, l1_ln3_b):
    B, S, D = x.shape
    assert B == 1 and S % MB_BLOCK == 0 and S % FFN_TM == 0
    bf = lambda w: w.astype(BF16)
    h = x.reshape(S, D)

    h, (hg_in, hg_out, f2_in, f2_out) = _ffn_ln(
        h, bf(l0_ffn1_in), bf(l0_ffn1_out), l0_ln1_g, l0_ln1_b,
        cast_weights=(l0_hg_in, l0_hg_out, l0_ffn2_in, l0_ffn2_out))
    proj = _proj(h, hg_in, F32)
    mix = _hgrn_core(proj, lb_logits, l0_hg_norm_g)
    h = _out_ln(h, mix, hg_out, l0_ln2_g, l0_ln2_b)
    h, (f1_in, f1_out) = _ffn_ln(h, f2_in, f2_out, l0_ln3_g, l0_ln3_b,
                                 cast_weights=(l1_ffn1_in, l1_ffn1_out))

    h, (mb_in, mb_out, f2_in, f2_out) = _ffn_ln(
        h, f1_in, f1_out, l1_ln1_g, l1_ln1_b,
        cast_weights=(l1_mb_in, l1_mb_out, l1_ffn2_in, l1_ffn2_out))
    qkv = _proj(h, mb_in, BF16)
    mix = _moba_core(qkv)
    h = _out_ln(h, mix, mb_out, l1_ln2_g, l1_ln2_b)
    h, _ = _ffn_ln(h, f2_in, f2_out, l1_ln3_g, l1_ln3_b)
    return h.reshape(B, S, D)
```

```python
import functools

import jax
import jax.numpy as jnp
from jax import lax
from jax.experimental import pallas as pl
from jax.experimental.pallas import tpu as pltpu

F32 = jnp.float32
BF16 = jnp.bfloat16

DEPTH = 2
ALPHA = float((2 * DEPTH) ** 0.25)
LN_EPS = 1e-5
RMS_EPS = 1e-6
LOG2_E = 1.4426950408889634
HEAD_DIM = 128
SUBLANES = 8
BF16_SUBLANES = 16
LANES = 128
MB_BLOCK = 256
MB_TOPK = 3
MB_UNROLL = 4
MB_HEADS = 2
MB_TILES = 1

VMEM_LIMIT_BYTES = 56 * 1024 * 1024

FFN_TM = 512
FFN_TF = 256
PROJ_TM = 1024
PROJ_TN = 512
OUT_TM = 512
OUT_TK = 1024
HG_TB = 256
HG_CHUNK = 64
HG_HB = 8
HG_SAFE_LOG = 60.0


def _params(*sem):
    return pltpu.CompilerParams(dimension_semantics=sem, vmem_limit_bytes=VMEM_LIMIT_BYTES)


LN_ROWS = 128
ACC_TN = 1024


def _layer_norm_inplace(o_ref, g_ref, b_ref):
    g = g_ref[...]
    b = b_ref[...]

    def body(r, carry):
        rows = pl.ds(pl.multiple_of(r * LN_ROWS, LN_ROWS), LN_ROWS)
        y = o_ref[rows, :]
        mu = jnp.mean(y, axis=-1, keepdims=True)
        yc = y - mu
        var = jnp.mean(yc * yc, axis=-1, keepdims=True)
        o_ref[rows, :] = yc * lax.rsqrt(var + LN_EPS) * g + b
        return carry

    lax.fori_loop(0, o_ref.shape[0] // LN_ROWS, body, 0)


def _accumulate_dot(o_ref, a, w_ref):
    tn = min(ACC_TN, o_ref.shape[1])
    assert o_ref.shape[1] % tn == 0
    for c in range(o_ref.shape[1] // tn):
        cols = slice(c * tn, (c + 1) * tn)
        o_ref[:, cols] += jnp.dot(a, w_ref[:, cols], preferred_element_type=F32)


def _ffn_ln_kernel(n_cast, x_ref, wg_ref, wu_ref, wo_ref, g_ref, b_ref, *refs):
    cast_in = refs[:n_cast]
    o_ref = refs[n_cast]
    cast_out = refs[n_cast + 1: 2 * n_cast + 1]
    xb_ref = refs[2 * n_cast + 1]
    j = pl.program_id(1)

    @pl.when(j == 0)
    def _():
        x = x_ref[...]
        xb_ref[...] = x.astype(BF16)
        o_ref[...] = ALPHA * x

    xb = xb_ref[...]
    gate = jnp.dot(xb, wg_ref[...], preferred_element_type=F32)
    up = jnp.dot(xb, wu_ref[...], preferred_element_type=F32)
    h = (0.5 * gate * jax.nn.sigmoid(gate) * up).astype(BF16)
    _accumulate_dot(o_ref, h, wo_ref)

    for src, dst in zip(cast_in, cast_out):
        dst[...] = src[...].astype(BF16)

    @pl.when(j == pl.num_programs(1) - 1)
    def _():
        _layer_norm_inplace(o_ref, g_ref, b_ref)


def _cast_tile_spec(w, ni, nj):
    R, C = w.shape
    rows = R // ni
    assert rows * ni == R and rows % BF16_SUBLANES == 0
    width = next(c for c in (LANES, 2 * LANES, 4 * LANES, 8 * LANES) if C % c == 0 and C // c <= nj)
    ncols = C // width
    return pl.BlockSpec((rows, width), lambda i, j: (i, jnp.minimum(j, ncols - 1)))


def _ffn_ln(x, w_in, w_out, g, b, cast_weights=()):
    S, D = x.shape
    F = w_out.shape[0]
    nf = F // FFN_TF
    ni = S // FFN_TM
    cast_specs = [_cast_tile_spec(w, ni, nf) for w in cast_weights]
    outs = pl.pallas_call(
        functools.partial(_ffn_ln_kernel, len(cast_weights)),
        grid=(ni, nf),
        in_specs=[
            pl.BlockSpec((FFN_TM, D), lambda i, j: (i, 0)),
            pl.BlockSpec((D, FFN_TF), lambda i, j: (0, j)),
            pl.BlockSpec((D, FFN_TF), lambda i, j: (0, j + nf)),
            pl.BlockSpec((FFN_TF, D), lambda i, j: (j, 0)),
            pl.BlockSpec((1, D), lambda i, j: (0, 0)),
            pl.BlockSpec((1, D), lambda i, j: (0, 0)),
        ] + cast_specs,
        out_specs=[pl.BlockSpec((FFN_TM, D), lambda i, j: (i, 0))] + cast_specs,
        out_shape=[jax.ShapeDtypeStruct((S, D), F32)]
                  + [jax.ShapeDtypeStruct(w.shape, BF16) for w in cast_weights],
        scratch_shapes=[pltpu.VMEM((FFN_TM, D), BF16)],
        compiler_params=_params("parallel", "arbitrary"),
        name="ffn_ln",
    )(x, w_in, w_in, w_out, g.reshape(1, D), b.reshape(1, D), *cast_weights)
    return outs[0], tuple(outs[1:])


def _proj_kernel(x_ref, w_ref, o_ref, xb_ref):
    @pl.when(pl.program_id(1) == 0)
    def _():
        xb_ref[...] = x_ref[...].astype(BF16)

    o_ref[...] = jnp.dot(xb_ref[...], w_ref[...], preferred_element_type=F32).astype(o_ref.dtype)


def _proj(x, w, out_dtype):
    S, D = x.shape
    N = w.shape[1]
    return pl.pallas_call(
        _proj_kernel,
        grid=(S // PROJ_TM, N // PROJ_TN),
        in_specs=[
            pl.BlockSpec((PROJ_TM, D), lambda i, j: (i, 0)),
            pl.BlockSpec((D, PROJ_TN), lambda i, j: (0, j)),
        ],
        out_specs=pl.BlockSpec((PROJ_TM, PROJ_TN), lambda i, j: (i, j)),
        out_shape=jax.ShapeDtypeStruct((S, N), out_dtype),
        scratch_shapes=[pltpu.VMEM((PROJ_TM, D), BF16)],
        compiler_params=_params("parallel", "arbitrary"),
        name="proj",
    )(x, w)


def _out_ln_kernel(x_ref, a_ref, w_ref, g_ref, b_ref, o_ref):
    k = pl.program_id(1)

    @pl.when(k == 0)
    def _():
        o_ref[...] = ALPHA * x_ref[...]

    _accumulate_dot(o_ref, a_ref[...], w_ref)

    @pl.when(k == pl.num_programs(1) - 1)
    def _():
        _layer_norm_inplace(o_ref, g_ref, b_ref)


def _out_ln(x, a, w, g, b):
    S, D = x.shape
    K = a.shape[1]
    return pl.pallas_call(
        _out_ln_kernel,
        grid=(S // OUT_TM, K // OUT_TK),
        in_specs=[
            pl.BlockSpec((OUT_TM, D), lambda i, k: (i, 0)),
            pl.BlockSpec((OUT_TM, OUT_TK), lambda i, k: (i, k)),
            pl.BlockSpec((OUT_TK, D), lambda i, k: (k, 0)),
            pl.BlockSpec((1, D), lambda i, k: (0, 0)),
            pl.BlockSpec((1, D), lambda i, k: (0, 0)),
        ],
        out_specs=pl.BlockSpec((OUT_TM, D), lambda i, k: (i, 0)),
        out_shape=jax.ShapeDtypeStruct((S, D), F32),
        compiler_params=_params("parallel", "arbitrary"),
        name="out_ln",
    )(x, a, w, g.reshape(1, D), b.reshape(1, D))


def _nt_dot(a, b):
    return lax.dot_general(a, b, (((1,), (1,)), ((), ())), preferred_element_type=F32)


def _tn_dot(a, b):
    return lax.dot_general(a, b, (((0,), (0,)), ((), ())), preferred_element_type=F32)


def _hgrn_kernel(q_ref, f_ref, v_ref, g_ref, lb_ref, ng_ref, o_ref,
                 st_ref, cum_ref, qs_ref, oi_ref):
    TB, W = q_ref.shape
    C = HG_CHUNK
    nchunk = TB // C
    nhead = W // HEAD_DIM

    @pl.when(pl.program_id(1) == 0)
    def _():
        st_ref[...] = jnp.zeros_like(st_ref)

    lg = lb_ref[...]
    ex = jnp.exp(lg - jnp.max(lg, axis=0, keepdims=True))
    lb = ex[0:1, :] / jnp.sum(ex, axis=0, keepdims=True)
    qv = q_ref[...]
    q = qv * jax.nn.sigmoid(qv)
    forget = lb + (1.0 - lb) * jax.nn.sigmoid(f_ref[...])
    k = 1.0 - forget
    logf = jnp.log(forget)

    r = lax.broadcasted_iota(jnp.int32, (TB, TB), 0)
    c = lax.broadcasted_iota(jnp.int32, (TB, TB), 1)
    tri = jnp.where((c <= r) & (c // C == r // C), 1.0, 0.0).astype(BF16)
    hi = logf.astype(BF16)
    r1 = logf - hi.astype(F32)
    mid = r1.astype(BF16)
    lo = (r1 - mid.astype(F32)).astype(BF16)
    cum = (jnp.dot(tri, hi, preferred_element_type=F32)
           + jnp.dot(tri, mid, preferred_element_type=F32)
           + jnp.dot(tri, lo, preferred_element_type=F32))

    bound = jnp.zeros((1, W), F32)
    for ci in range(nchunk):
        mrow = cum[ci * C + C // 2 - 1: ci * C + C // 2, :]
        lrow = cum[ci * C + C - 1: ci * C + C, :]
        bound = jnp.maximum(bound, jnp.maximum(-mrow, mrow - lrow))
    safe = jnp.max(bound) <= HG_SAFE_LOG

    tr = lax.broadcasted_iota(jnp.int32, (C, C), 0)
    ts = lax.broadcasted_iota(jnp.int32, (C, C), 1)
    causal = ts <= tr

    gv = g_ref[...]
    og = ng_ref[...] * (gv * jax.nn.sigmoid(gv))

    def factorised_intra(rows, lanes):
        cu = cum[rows, lanes]
        m = cu[C // 2 - 1: C // 2, :]
        qd = (q[rows, lanes] * jnp.exp(cu - m)).astype(BF16)
        kd = (k[rows, lanes] * jnp.exp(m - cu)).astype(BF16)
        a = jnp.where(causal, _nt_dot(qd, kd), 0.0).astype(BF16)
        return jnp.dot(a, v_ref[rows, lanes].astype(BF16), preferred_element_type=F32)

    def stored_intra(rows, lanes):
        return oi_ref[rows, lanes]

    def recurrence(intra):
        for h in range(nhead):
            lanes = slice(h * HEAD_DIM, (h + 1) * HEAD_DIM)
            st = st_ref[h]
            for ci in range(nchunk):
                rows = slice(ci * C, (ci + 1) * C)
                cu = cum[rows, lanes]
                last = cu[C - 1: C, :]
                qe = (q[rows, lanes] * jnp.exp(cu)).astype(BF16)
                o = intra(rows, lanes) + _nt_dot(qe, st.astype(BF16))
                ke = (k[rows, lanes] * jnp.exp(last - cu)).astype(BF16)
                st = st * jnp.exp(last) + _tn_dot(v_ref[rows, lanes].astype(BF16), ke)
                o = o * lax.rsqrt(jnp.mean(o * o, axis=-1, keepdims=True) + RMS_EPS)
                o_ref[rows, lanes] = (o * og[rows, lanes]).astype(o_ref.dtype)
            st_ref[h] = st

    @pl.when(safe)
    def _():
        recurrence(factorised_intra)

    @pl.when(jnp.logical_not(safe))
    def _():
        cum_ref[...] = cum
        qs_ref[...] = q
        sidx = lax.broadcasted_iota(jnp.int32, (C, W), 0)
        for ci in range(nchunk):
            rows = slice(ci * C, (ci + 1) * C)
            cu = cum[rows, :]
            kc = k[rows, :]
            vc = v_ref[rows, :]

            def rows_body(t8, carry, ci=ci, cu=cu, kc=kc, vc=vc):
                base = pl.multiple_of(ci * C + t8 * SUBLANES, SUBLANES)
                c8 = cum_ref[pl.ds(base, SUBLANES), :]
                q8 = qs_ref[pl.ds(base, SUBLANES), :]
                out_rows = []
                for r in range(SUBLANES):
                    t = t8 * SUBLANES + r
                    dec = jnp.exp(jnp.where(sidx <= t, c8[r:r + 1, :] - cu, -jnp.inf))
                    prod = q8[r:r + 1, :] * dec * kc
                    heads = []
                    for h in range(nhead):
                        lanes = slice(h * HEAD_DIM, (h + 1) * HEAD_DIM)
                        sc = jnp.sum(prod[:, lanes], axis=1, keepdims=True)
                        heads.append(jnp.sum(sc * vc[:, lanes], axis=0, keepdims=True))
                    out_rows.append(jnp.concatenate(heads, axis=1))
                oi_ref[pl.ds(base, SUBLANES), :] = jnp.concatenate(out_rows, axis=0)
                return carry

            lax.fori_loop(0, C // SUBLANES, rows_body, 0)
        recurrence(stored_intra)


def _hgrn_core(proj, lb_logits, norm_g):
    S, D4 = proj.shape
    D = D4 // 4
    W = HG_HB * HEAD_DIM
    ng = D // W
    return pl.pallas_call(
        _hgrn_kernel,
        grid=(ng, S // HG_TB),
        in_specs=[
            pl.BlockSpec((HG_TB, W), lambda hg, t: (t, hg)),
            pl.BlockSpec((HG_TB, W), lambda hg, t: (t, hg + ng)),
            pl.BlockSpec((HG_TB, W), lambda hg, t: (t, hg + 2 * ng)),
            pl.BlockSpec((HG_TB, W), lambda hg, t: (t, hg + 3 * ng)),
            pl.BlockSpec((lb_logits.shape[0], W), lambda hg, t: (0, hg)),
            pl.BlockSpec((1, W), lambda hg, t: (0, hg)),
        ],
        out_specs=pl.BlockSpec((HG_TB, W), lambda hg, t: (t, hg)),
        out_shape=jax.ShapeDtypeStruct((S, D), BF16),
        scratch_shapes=[
            pltpu.VMEM((HG_HB, HEAD_DIM, HEAD_DIM), F32),
            pltpu.VMEM((HG_TB, W), F32),
            pltpu.VMEM((HG_TB, W), F32),
            pltpu.VMEM((HG_TB, W), F32),
        ],
        compiler_params=_params("parallel", "arbitrary"),
        name="hgrn_core",
    )(proj, proj, proj, proj, lb_logits.astype(F32), norm_g.reshape(1, D))


def _moba_kernel(q_ref, k_ref, v_ref, o_ref, km_ref, sel_ref, s_ref, vt_ref):
    L = MB_BLOCK
    S = k_ref.shape[0]
    nb = S // L
    c2 = HEAD_DIM ** -0.5 * LOG2_E
    last_group = nb // MB_UNROLL - 1

    def mean_body(n, carry):
        rows = pl.ds(pl.multiple_of(n * L, L), L)
        for h in range(MB_HEADS):
            lanes = slice(h * HEAD_DIM, (h + 1) * HEAD_DIM)
            km_ref[h, pl.ds(n, 1), :] = jnp.mean(k_ref[rows, lanes].astype(F32), axis=0, keepdims=True)
            vt_ref[h, n] = v_ref[rows, lanes].astype(F32).T.astype(BF16)
        return carry
    lax.fori_loop(0, nb, mean_body, 0)

    def tiles_body(ti, carry):
        _moba_tiles(ti * MB_TILES, q_ref, k_ref, vt_ref, o_ref, km_ref, sel_ref, s_ref, c2, last_group)
        return carry
    lax.fori_loop(0, nb // MB_TILES, tiles_body, 0)


def _moba_tiles(i0, q_ref, k_ref, vt_ref, o_ref, km_ref, sel_ref, s_ref, c2, last_group):
    L = MB_BLOCK
    nb = k_ref.shape[0] // L
    units = [(h, t) for t in range(MB_TILES) for h in range(MB_HEADS)]
    ids = range(len(units))
    lanes = [slice(h * HEAD_DIM, (h + 1) * HEAD_DIM) for h, _ in units]
    blk = [i0 + t for _, t in units]
    own = [pl.ds(pl.multiple_of(b * L, L), L) for b in blk]
    q_raw = [q_ref[own[u], lanes[u]] for u in ids]
    q = [(x.astype(F32) * c2).astype(BF16) for x in q_raw]

    nidx = lax.broadcasted_iota(jnp.int32, (nb, L), 0)
    for u in ids:
        gate = _nt_dot(km_ref[units[u][0]].astype(BF16), q_raw[u])
        gate = jnp.where(nidx < blk[u], gate, -jnp.inf)
        sel = jnp.zeros((nb, L), F32)
        for _ in range(min(MB_TOPK, nb)):
            mx = jnp.max(gate, axis=0, keepdims=True)
            first = jnp.min(jnp.where(gate == mx, nidx, nb), axis=0, keepdims=True)
            pick = (nidx == first) & (mx > -jnp.inf)
            sel = jnp.where(pick, 1.0, sel)
            gate = jnp.where(pick, -jnp.inf, gate)
        sel_ref[u] = sel

    kpos = lax.broadcasted_iota(jnp.int32, (L, L), 0)
    qpos = lax.broadcasted_iota(jnp.int32, (L, L), 1)
    m0, l0, acc0 = [], [], []
    for u in ids:
        st = _nt_dot(k_ref[own[u], lanes[u]], q[u])
        st = jnp.where(kpos <= qpos, st, -jnp.inf)
        mu = jnp.max(st, axis=0, keepdims=True)
        p = jnp.exp2(st - mu)
        m0.append(mu)
        l0.append(jnp.sum(p, axis=0, keepdims=True))
        acc0.append(jnp.dot(vt_ref[units[u][0], blk[u]], p.astype(BF16),
                            preferred_element_type=F32))

    def block_rows(g, j):
        n = g * MB_UNROLL + j
        return n, pl.ds(pl.multiple_of(n * L, L), L)

    def score_group(g, slot):
        colmax = [jnp.full((1, L), -jnp.inf, F32) for _ in ids]
        for j in range(MB_UNROLL):
            n, r = block_rows(g, j)
            for u in ids:
                s = _nt_dot(k_ref[r, lanes[u]], q[u])
                s_ref[u, slot, j] = s
                on = sel_ref[u, pl.ds(n, 1), :] > 0.0
                colmax[u] = jnp.maximum(
                    colmax[u], jnp.where(on, jnp.max(s, axis=0, keepdims=True), -jnp.inf))
        return colmax

    def attend_group(g, slot, colmax, m, l, acc):
        m, l, acc = list(m), list(l), list(acc)
        for u in ids:
            m_new = jnp.maximum(m[u], colmax[u])
            alpha = jnp.exp2(m[u] - m_new)
            lu = alpha * l[u]
            au = alpha * acc[u]
            for j in range(MB_UNROLL):
                n, _ = block_rows(g, j)
                on = sel_ref[u, pl.ds(n, 1), :] > 0.0
                p = jnp.exp2(s_ref[u, slot, j] - jnp.where(on, m_new, jnp.inf))
                lu = lu + jnp.sum(p, axis=0, keepdims=True)
                au = au + jnp.dot(vt_ref[units[u][0], n], p.astype(BF16), preferred_element_type=F32)
            m[u], l[u], acc[u] = m_new, lu, au
        return m, l, acc

    def pair_body(pi, carry):
        m, l, acc, cm_even = carry
        g = 2 * pi
        cm_odd = score_group(g + 1, 1)
        m, l, acc = attend_group(g, 0, cm_even, m, l, acc)
        cm_even = score_group(jnp.minimum(g + 2, last_group), 0)
        m, l, acc = attend_group(g + 1, 1, cm_odd, m, l, acc)
        return m, l, acc, cm_even

    ngroups = (i0 + MB_TILES - 1 + MB_UNROLL - 1) // MB_UNROLL
    m, l, acc, cm_even = lax.fori_loop(0, ngroups // 2, pair_body,
                                       (m0, l0, acc0, score_group(0, 0)))
    m, l, acc = lax.cond(ngroups % 2 == 1,
                         lambda: attend_group(ngroups - 1, 0, cm_even, m, l, acc),
                         lambda: (list(m), list(l), list(acc)))
    for u in ids:
        o_ref[own[u], lanes[u]] = (acc[u] / l[u]).T.astype(o_ref.dtype)


def _moba_core(qkv):
    S, D3 = qkv.shape
    D = D3 // 3
    L = MB_BLOCK
    nb = S // L
    W = MB_HEADS * HEAD_DIM
    ng = D // W
    nu = MB_HEADS * MB_TILES
    assert nb % (2 * MB_UNROLL) == 0 and nb % MB_TILES == 0 and ng * W == D
    return pl.pallas_call(
        _moba_kernel,
        grid=(ng,),
        in_specs=[
            pl.BlockSpec((S, W), lambda hg: (0, hg)),
            pl.BlockSpec((S, W), lambda hg: (0, ng + hg)),
            pl.BlockSpec((S, W), lambda hg: (0, 2 * ng + hg)),
        ],
        out_specs=pl.BlockSpec((S, W), lambda hg: (0, hg)),
        out_shape=jax.ShapeDtypeStruct((S, D), BF16),
        scratch_shapes=[
            pltpu.VMEM((MB_HEADS, nb, HEAD_DIM), F32),
            pltpu.VMEM((nu, nb, L), F32),
            pltpu.VMEM((nu, 2, MB_UNROLL, L, L), F32),
            pltpu.VMEM((MB_HEADS, nb, HEAD_DIM, L), BF16),
        ],
        compiler_params=_params("parallel"),
        name="moba_core",
    )(qkv, qkv, qkv)


def kernel(x, lb_logits, l0_ffn1_in, l0_ffn1_out, l0_ln1_g, l0_ln1_b, l0_hg_in, l0_hg_norm_g, l0_hg_out, l0_ln2_g, l0_ln2_b, l0_ffn2_in, l0_ffn2_out, l0_ln3_g, l0_ln3_b, l1_ffn1_in, l1_ffn1_out, l1_ln1_g, l1_ln1_b, l1_mb_in, l1_mb_out, l1_ln2_g, l1_ln2_b, l1_ffn2_in, l1_ffn2_out, l1_ln3_g, l1_ln3_b):
    B, S, D = x.shape
    assert B == 1 and S % MB_BLOCK == 0 and S % FFN_TM == 0
    bf = lambda w: w.astype(BF16)
    h = x.reshape(S, D)

    h, (hg_in, hg_out, f2_in, f2_out) = _ffn_ln(
        h, bf(l0_ffn1_in), bf(l0_ffn1_out), l0_ln1_g, l0_ln1_b,
        cast_weights=(l0_hg_in, l0_hg_out, l0_ffn2_in, l0_ffn2_out))
    proj = _proj(h, hg_in, F32)
    mix = _hgrn_core(proj, lb_logits, l0_hg_norm_g)
    h = _out_ln(h, mix, hg_out, l0_ln2_g, l0_ln2_b)
    h, (f1_in, f1_out) = _ffn_ln(h, f2_in, f2_out, l0_ln3_g, l0_ln3_b,
                                 cast_weights=(l1_ffn1_in, l1_ffn1_out))

    h, (mb_in, mb_out, f2_in, f2_out) = _ffn_ln(
        h, f1_in, f1_out, l1_ln1_g, l1_ln1_b,
        cast_weights=(l1_mb_in, l1_mb_out, l1_ffn2_in, l1_ffn2_out))
    qkv = _proj(h, mb_in, BF16)
    mix = _moba_core(qkv)
    h = _out_ln(h, mix, mb_out, l1_ln2_g, l1_ln2_b)
    h, _ = _ffn_ln(h, f2_in, f2_out, l1_ln3_g, l1_ln3_b)
    return h.reshape(B, S, D)
```

```python
import functools

import jax
import jax.numpy as jnp
from jax import lax
from jax.experimental import pallas as pl
from jax.experimental.pallas import tpu as pltpu

F32 = jnp.float32
BF16 = jnp.bfloat16

DEPTH = 2
ALPHA = float((2 * DEPTH) ** 0.25)
LN_EPS = 1e-5
RMS_EPS = 1e-6
LOG2_E = 1.4426950408889634
HEAD_DIM = 128
SUBLANES = 8
BF16_SUBLANES = 16
LANES = 128
MB_BLOCK = 256
MB_TOPK = 3
MB_UNROLL = 4
MB_HEADS = 2

VMEM_LIMIT_BYTES = 56 * 1024 * 1024

FFN_TM = 512
FFN_TF = 256
PROJ_TM = 1024
PROJ_TN = 512
OUT_TM = 512
OUT_TK = 1024
HG_TB = 256
HG_CHUNK = 64
HG_HB = 8
HG_SAFE_LOG = 60.0


def _params(*sem):
    return pltpu.CompilerParams(dimension_semantics=sem, vmem_limit_bytes=VMEM_LIMIT_BYTES)


LN_ROWS = 128
ACC_TN = 1024


def _layer_norm_inplace(o_ref, g_ref, b_ref):
    g = g_ref[...]
    b = b_ref[...]

    def body(r, carry):
        rows = pl.ds(pl.multiple_of(r * LN_ROWS, LN_ROWS), LN_ROWS)
        y = o_ref[rows, :]
        mu = jnp.mean(y, axis=-1, keepdims=True)
        yc = y - mu
        var = jnp.mean(yc * yc, axis=-1, keepdims=True)
        o_ref[rows, :] = yc * lax.rsqrt(var + LN_EPS) * g + b
        return carry

    lax.fori_loop(0, o_ref.shape[0] // LN_ROWS, body, 0)


def _accumulate_dot(o_ref, a, w_ref):
    tn = min(ACC_TN, o_ref.shape[1])
    assert o_ref.shape[1] % tn == 0
    for c in range(o_ref.shape[1] // tn):
        cols = slice(c * tn, (c + 1) * tn)
        o_ref[:, cols] += jnp.dot(a, w_ref[:, cols], preferred_element_type=F32)


def _ffn_ln_kernel(n_cast, x_ref, wg_ref, wu_ref, wo_ref, g_ref, b_ref, *refs):
    cast_in = refs[:n_cast]
    o_ref = refs[n_cast]
    cast_out = refs[n_cast + 1: 2 * n_cast + 1]
    xb_ref = refs[2 * n_cast + 1]
    j = pl.program_id(1)

    @pl.when(j == 0)
    def _():
        x = x_ref[...]
        xb_ref[...] = x.astype(BF16)
        o_ref[...] = ALPHA * x

    xb = xb_ref[...]
    gate = jnp.dot(xb, wg_ref[...], preferred_element_type=F32)
    up = jnp.dot(xb, wu_ref[...], preferred_element_type=F32)
    h = (0.5 * gate * jax.nn.sigmoid(gate) * up).astype(BF16)
    _accumulate_dot(o_ref, h, wo_ref)

    for src, dst in zip(cast_in, cast_out):
        dst[...] = src[...].astype(BF16)

    @pl.when(j == pl.num_programs(1) - 1)
    def _():
        _layer_norm_inplace(o_ref, g_ref, b_ref)


def _cast_tile_spec(w, ni, nj):
    R, C = w.shape
    rows = R // ni
    assert rows * ni == R and rows % BF16_SUBLANES == 0
    width = next(c for c in (LANES, 2 * LANES, 4 * LANES, 8 * LANES) if C % c == 0 and C // c <= nj)
    ncols = C // width
    return pl.BlockSpec((rows, width), lambda i, j: (i, jnp.minimum(j, ncols - 1)))


def _ffn_ln(x, w_in, w_out, g, b, cast_weights=()):
    S, D = x.shape
    F = w_out.shape[0]
    nf = F // FFN_TF
    ni = S // FFN_TM
    cast_specs = [_cast_tile_spec(w, ni, nf) for w in cast_weights]
    outs = pl.pallas_call(
        functools.partial(_ffn_ln_kernel, len(cast_weights)),
        grid=(ni, nf),
        in_specs=[
            pl.BlockSpec((FFN_TM, D), lambda i, j: (i, 0)),
            pl.BlockSpec((D, FFN_TF), lambda i, j: (0, j)),
            pl.BlockSpec((D, FFN_TF), lambda i, j: (0, j + nf)),
            pl.BlockSpec((FFN_TF, D), lambda i, j: (j, 0)),
            pl.BlockSpec((1, D), lambda i, j: (0, 0)),
            pl.BlockSpec((1, D), lambda i, j: (0, 0)),
        ] + cast_specs,
        out_specs=[pl.BlockSpec((FFN_TM, D), lambda i, j: (i, 0))] + cast_specs,
        out_shape=[jax.ShapeDtypeStruct((S, D), F32)]
                  + [jax.ShapeDtypeStruct(w.shape, BF16) for w in cast_weights],
        scratch_shapes=[pltpu.VMEM((FFN_TM, D), BF16)],
        compiler_params=_params("parallel", "arbitrary"),
        name="ffn_ln",
    )(x, w_in, w_in, w_out, g.reshape(1, D), b.reshape(1, D), *cast_weights)
    return outs[0], tuple(outs[1:])


def _proj_kernel(x_ref, w_ref, o_ref, xb_ref):
    @pl.when(pl.program_id(1) == 0)
    def _():
        xb_ref[...] = x_ref[...].astype(BF16)

    o_ref[...] = jnp.dot(xb_ref[...], w_ref[...], preferred_element_type=F32).astype(o_ref.dtype)


def _proj(x, w, out_dtype):
    S, D = x.shape
    N = w.shape[1]
    return pl.pallas_call(
        _proj_kernel,
        grid=(S // PROJ_TM, N // PROJ_TN),
        in_specs=[
            pl.BlockSpec((PROJ_TM, D), lambda i, j: (i, 0)),
            pl.BlockSpec((D, PROJ_TN), lambda i, j: (0, j)),
        ],
        out_specs=pl.BlockSpec((PROJ_TM, PROJ_TN), lambda i, j: (i, j)),
        out_shape=jax.ShapeDtypeStruct((S, N), out_dtype),
        scratch_shapes=[pltpu.VMEM((PROJ_TM, D), BF16)],
        compiler_params=_params("parallel", "arbitrary"),
        name="proj",
    )(x, w)


def _out_ln_kernel(x_ref, a_ref, w_ref, g_ref, b_ref, o_ref):
    k = pl.program_id(1)

    @pl.when(k == 0)
    def _():
        o_ref[...] = ALPHA * x_ref[...]

    _accumulate_dot(o_ref, a_ref[...], w_ref)

    @pl.when(k == pl.num_programs(1) - 1)
    def _():
        _layer_norm_inplace(o_ref, g_ref, b_ref)


def _out_ln(x, a, w, g, b):
    S, D = x.shape
    K = a.shape[1]
    return pl.pallas_call(
        _out_ln_kernel,
        grid=(S // OUT_TM, K // OUT_TK),
        in_specs=[
            pl.BlockSpec((OUT_TM, D), lambda i, k: (i, 0)),
            pl.BlockSpec((OUT_TM, OUT_TK), lambda i, k: (i, k)),
            pl.BlockSpec((OUT_TK, D), lambda i, k: (k, 0)),
            pl.BlockSpec((1, D), lambda i, k: (0, 0)),
            pl.BlockSpec((1, D), lambda i, k: (0, 0)),
        ],
        out_specs=pl.BlockSpec((OUT_TM, D), lambda i, k: (i, 0)),
        out_shape=jax.ShapeDtypeStruct((S, D), F32),
        compiler_params=_params("parallel", "arbitrary"),
        name="out_ln",
    )(x, a, w, g.reshape(1, D), b.reshape(1, D))


def _nt_dot(a, b):
    return lax.dot_general(a, b, (((1,), (1,)), ((), ())), preferred_element_type=F32)


def _tn_dot(a, b):
    return lax.dot_general(a, b, (((0,), (0,)), ((), ())), preferred_element_type=F32)


def _hgrn_kernel(q_ref, f_ref, v_ref, g_ref, lb_ref, ng_ref, o_ref,
                 st_ref, cum_ref, qs_ref, oi_ref):
    TB, W = q_ref.shape
    C = HG_CHUNK
    nchunk = TB // C
    nhead = W // HEAD_DIM

    @pl.when(pl.program_id(1) == 0)
    def _():
        st_ref[...] = jnp.zeros_like(st_ref)

    lg = lb_ref[...]
    ex = jnp.exp(lg - jnp.max(lg, axis=0, keepdims=True))
    lb = ex[0:1, :] / jnp.sum(ex, axis=0, keepdims=True)
    qv = q_ref[...]
    q = qv * jax.nn.sigmoid(qv)
    forget = lb + (1.0 - lb) * jax.nn.sigmoid(f_ref[...])
    k = 1.0 - forget
    logf = jnp.log(forget)

    r = lax.broadcasted_iota(jnp.int32, (TB, TB), 0)
    c = lax.broadcasted_iota(jnp.int32, (TB, TB), 1)
    tri = jnp.where((c <= r) & (c // C == r // C), 1.0, 0.0).astype(BF16)
    hi = logf.astype(BF16)
    r1 = logf - hi.astype(F32)
    mid = r1.astype(BF16)
    lo = (r1 - mid.astype(F32)).astype(BF16)
    cum = (jnp.dot(tri, hi, preferred_element_type=F32)
           + jnp.dot(tri, mid, preferred_element_type=F32)
           + jnp.dot(tri, lo, preferred_element_type=F32))

    bound = jnp.zeros((1, W), F32)
    for ci in range(nchunk):
        mrow = cum[ci * C + C // 2 - 1: ci * C + C // 2, :]
        lrow = cum[ci * C + C - 1: ci * C + C, :]
        bound = jnp.maximum(bound, jnp.maximum(-mrow, mrow - lrow))
    safe = jnp.max(bound) <= HG_SAFE_LOG

    tr = lax.broadcasted_iota(jnp.int32, (C, C), 0)
    ts = lax.broadcasted_iota(jnp.int32, (C, C), 1)
    causal = ts <= tr

    gv = g_ref[...]
    og = ng_ref[...] * (gv * jax.nn.sigmoid(gv))

    def factorised_intra(rows, lanes):
        cu = cum[rows, lanes]
        m = cu[C // 2 - 1: C // 2, :]
        qd = (q[rows, lanes] * jnp.exp(cu - m)).astype(BF16)
        kd = (k[rows, lanes] * jnp.exp(m - cu)).astype(BF16)
        a = jnp.where(causal, _nt_dot(qd, kd), 0.0).astype(BF16)
        return jnp.dot(a, v_ref[rows, lanes].astype(BF16), preferred_element_type=F32)

    def stored_intra(rows, lanes):
        return oi_ref[rows, lanes]

    def recurrence(intra):
        for h in range(nhead):
            lanes = slice(h * HEAD_DIM, (h + 1) * HEAD_DIM)
            st = st_ref[h]
            for ci in range(nchunk):
                rows = slice(ci * C, (ci + 1) * C)
                cu = cum[rows, lanes]
                last = cu[C - 1: C, :]
                qe = (q[rows, lanes] * jnp.exp(cu)).astype(BF16)
                o = intra(rows, lanes) + _nt_dot(qe, st.astype(BF16))
                ke = (k[rows, lanes] * jnp.exp(last - cu)).astype(BF16)
                st = st * jnp.exp(last) + _tn_dot(v_ref[rows, lanes].astype(BF16), ke)
                o = o * lax.rsqrt(jnp.mean(o * o, axis=-1, keepdims=True) + RMS_EPS)
                o_ref[rows, lanes] = (o * og[rows, lanes]).astype(o_ref.dtype)
            st_ref[h] = st

    @pl.when(safe)
    def _():
        recurrence(factorised_intra)

    @pl.when(jnp.logical_not(safe))
    def _():
        cum_ref[...] = cum
        qs_ref[...] = q
        sidx = lax.broadcasted_iota(jnp.int32, (C, W), 0)
        for ci in range(nchunk):
            rows = slice(ci * C, (ci + 1) * C)
            cu = cum[rows, :]
            kc = k[rows, :]
            vc = v_ref[rows, :]

            def rows_body(t8, carry, ci=ci, cu=cu, kc=kc, vc=vc):
                base = pl.multiple_of(ci * C + t8 * SUBLANES, SUBLANES)
                c8 = cum_ref[pl.ds(base, SUBLANES), :]
                q8 = qs_ref[pl.ds(base, SUBLANES), :]
                out_rows = []
                for r in range(SUBLANES):
                    t = t8 * SUBLANES + r
                    dec = jnp.exp(jnp.where(sidx <= t, c8[r:r + 1, :] - cu, -jnp.inf))
                    prod = q8[r:r + 1, :] * dec * kc
                    heads = []
                    for h in range(nhead):
                        lanes = slice(h * HEAD_DIM, (h + 1) * HEAD_DIM)
                        sc = jnp.sum(prod[:, lanes], axis=1, keepdims=True)
                        heads.append(jnp.sum(sc * vc[:, lanes], axis=0, keepdims=True))
                    out_rows.append(jnp.concatenate(heads, axis=1))
                oi_ref[pl.ds(base, SUBLANES), :] = jnp.concatenate(out_rows, axis=0)
                return carry

            lax.fori_loop(0, C // SUBLANES, rows_body, 0)
        recurrence(stored_intra)


def _hgrn_core(proj, lb_logits, norm_g):
    S, D4 = proj.shape
    D = D4 // 4
    W = HG_HB * HEAD_DIM
    ng = D // W
    return pl.pallas_call(
        _hgrn_kernel,
        grid=(ng, S // HG_TB),
        in_specs=[
            pl.BlockSpec((HG_TB, W), lambda hg, t: (t, hg)),
            pl.BlockSpec((HG_TB, W), lambda hg, t: (t, hg + ng)),
            pl.BlockSpec((HG_TB, W), lambda hg, t: (t, hg + 2 * ng)),
            pl.BlockSpec((HG_TB, W), lambda hg, t: (t, hg + 3 * ng)),
            pl.BlockSpec((lb_logits.shape[0], W), lambda hg, t: (0, hg)),
            pl.BlockSpec((1, W), lambda hg, t: (0, hg)),
        ],
        out_specs=pl.BlockSpec((HG_TB, W), lambda hg, t: (t, hg)),
        out_shape=jax.ShapeDtypeStruct((S, D), BF16),
        scratch_shapes=[
            pltpu.VMEM((HG_HB, HEAD_DIM, HEAD_DIM), F32),
            pltpu.VMEM((HG_TB, W), F32),
            pltpu.VMEM((HG_TB, W), F32),
            pltpu.VMEM((HG_TB, W), F32),
        ],
        compiler_params=_params("parallel", "arbitrary"),
        name="hgrn_core",
    )(proj, proj, proj, proj, lb_logits.astype(F32), norm_g.reshape(1, D))


def _moba_kernel(q_ref, k_ref, v_ref, o_ref, km_ref, sel_ref, s_ref, vt_ref):
    L = MB_BLOCK
    S = k_ref.shape[0]
    nb = S // L
    c2 = HEAD_DIM ** -0.5 * LOG2_E
    last_group = nb // MB_UNROLL - 1

    def mean_body(n, carry):
        rows = pl.ds(pl.multiple_of(n * L, L), L)
        for h in range(MB_HEADS):
            lanes = slice(h * HEAD_DIM, (h + 1) * HEAD_DIM)
            km_ref[h, pl.ds(n, 1), :] = jnp.mean(k_ref[rows, lanes].astype(F32), axis=0, keepdims=True)
            vt_ref[h, n] = v_ref[rows, lanes].astype(F32).T.astype(BF16)
        return carry
    lax.fori_loop(0, nb, mean_body, 0)

    def tile_body(i, carry):
        _moba_tile(i, q_ref, k_ref, vt_ref, o_ref, km_ref, sel_ref, s_ref, c2, last_group)
        return carry
    lax.fori_loop(0, nb, tile_body, 0)


def _moba_tile(i, q_ref, k_ref, vt_ref, o_ref, km_ref, sel_ref, s_ref, c2, last_group):
    L = MB_BLOCK
    nb = k_ref.shape[0] // L
    heads = range(MB_HEADS)
    lanes = [slice(h * HEAD_DIM, (h + 1) * HEAD_DIM) for h in heads]
    own = pl.ds(pl.multiple_of(i * L, L), L)
    q_raw = [q_ref[own, lanes[h]] for h in heads]
    q = [(x.astype(F32) * c2).astype(BF16) for x in q_raw]

    nidx = lax.broadcasted_iota(jnp.int32, (nb, L), 0)
    for h in heads:
        gate = _nt_dot(km_ref[h].astype(BF16), q_raw[h])
        gate = jnp.where(nidx < i, gate, -jnp.inf)
        sel = jnp.zeros((nb, L), F32)
        for _ in range(min(MB_TOPK, nb)):
            mx = jnp.max(gate, axis=0, keepdims=True)
            first = jnp.min(jnp.where(gate == mx, nidx, nb), axis=0, keepdims=True)
            pick = (nidx == first) & (mx > -jnp.inf)
            sel = jnp.where(pick, 1.0, sel)
            gate = jnp.where(pick, -jnp.inf, gate)
        sel_ref[h] = sel

    kpos = lax.broadcasted_iota(jnp.int32, (L, L), 0)
    qpos = lax.broadcasted_iota(jnp.int32, (L, L), 1)
    m0, l0, acc0 = [], [], []
    for h in heads:
        st = _nt_dot(k_ref[own, lanes[h]], q[h])
        st = jnp.where(kpos <= qpos, st, -jnp.inf)
        mh = jnp.max(st, axis=0, keepdims=True)
        p = jnp.exp2(st - mh)
        m0.append(mh)
        l0.append(jnp.sum(p, axis=0, keepdims=True))
        acc0.append(jnp.dot(vt_ref[h, i], p.astype(BF16), preferred_element_type=F32))

    def block_rows(g, u):
        n = g * MB_UNROLL + u
        return n, pl.ds(pl.multiple_of(n * L, L), L)

    def score_group(g, slot):
        colmax = [jnp.full((1, L), -jnp.inf, F32) for _ in heads]
        for u in range(MB_UNROLL):
            n, r = block_rows(g, u)
            for h in heads:
                s = _nt_dot(k_ref[r, lanes[h]], q[h])
                s_ref[h, slot, u] = s
                on = sel_ref[h, pl.ds(n, 1), :] > 0.0
                colmax[h] = jnp.maximum(
                    colmax[h], jnp.where(on, jnp.max(s, axis=0, keepdims=True), -jnp.inf))
        return colmax

    def attend_group(g, slot, colmax, m, l, acc):
        m, l, acc = list(m), list(l), list(acc)
        for h in heads:
            m_new = jnp.maximum(m[h], colmax[h])
            alpha = jnp.exp2(m[h] - m_new)
            lh = alpha * l[h]
            ah = alpha * acc[h]
            for u in range(MB_UNROLL):
                n, _ = block_rows(g, u)
                on = sel_ref[h, pl.ds(n, 1), :] > 0.0
                p = jnp.exp2(s_ref[h, slot, u] - jnp.where(on, m_new, jnp.inf))
                lh = lh + jnp.sum(p, axis=0, keepdims=True)
                ah = ah + jnp.dot(vt_ref[h, n], p.astype(BF16), preferred_element_type=F32)
            m[h], l[h], acc[h] = m_new, lh, ah
        return m, l, acc

    def pair_body(pi, carry):
        m, l, acc, cm_even = carry
        g = 2 * pi
        cm_odd = score_group(g + 1, 1)
        m, l, acc = attend_group(g, 0, cm_even, m, l, acc)
        cm_even = score_group(jnp.minimum(g + 2, last_group), 0)
        m, l, acc = attend_group(g + 1, 1, cm_odd, m, l, acc)
        return m, l, acc, cm_even

    ngroups = (i + MB_UNROLL - 1) // MB_UNROLL
    m, l, acc, cm_even = lax.fori_loop(0, ngroups // 2, pair_body,
                                       (m0, l0, acc0, score_group(0, 0)))
    m, l, acc = lax.cond(ngroups % 2 == 1,
                         lambda: attend_group(ngroups - 1, 0, cm_even, m, l, acc),
                         lambda: (list(m), list(l), list(acc)))
    for h in heads:
        o_ref[own, lanes[h]] = (acc[h] / l[h]).T.astype(o_ref.dtype)


def _moba_core(qkv):
    S, D3 = qkv.shape
    D = D3 // 3
    L = MB_BLOCK
    nb = S // L
    W = MB_HEADS * HEAD_DIM
    ng = D // W
    assert nb % (2 * MB_UNROLL) == 0 and ng * W == D
    return pl.pallas_call(
        _moba_kernel,
        grid=(ng,),
        in_specs=[
            pl.BlockSpec((S, W), lambda hg: (0, hg)),
            pl.BlockSpec((S, W), lambda hg: (0, ng + hg)),
            pl.BlockSpec((S, W), lambda hg: (0, 2 * ng + hg)),
        ],
        out_specs=pl.BlockSpec((S, W), lambda hg: (0, hg)),
        out_shape=jax.ShapeDtypeStruct((S, D), BF16),
        scratch_shapes=[
            pltpu.VMEM((MB_HEADS, nb, HEAD_DIM), F32),
            pltpu.VMEM((MB_HEADS, nb, L), F32),
            pltpu.VMEM((MB_HEADS, 2, MB_UNROLL, L, L), F32),
            pltpu.VMEM((MB_HEADS, nb, HEAD_DIM, L), BF16),
        ],
        compiler_params=_params("parallel"),
        name="moba_core",
    )(qkv, qkv, qkv)


def kernel(x, lb_logits, l0_ffn1_in, l0_ffn1_out, l0_ln1_g, l0_ln1_b, l0_hg_in, l0_hg_norm_g, l0_hg_out, l0_ln2_g, l0_ln2_b, l0_ffn2_in, l0_ffn2_out, l0_ln3_g, l0_ln3_b, l1_ffn1_in, l1_ffn1_out, l1_ln1_g, l1_ln1_b, l1_mb_in, l1_mb_out, l1_ln2_g, l1_ln2_b, l1_ffn2_in, l1_ffn2_out, l1_ln3_g, l1_ln3_b):
    B, S, D = x.shape
    assert B == 1 and S % MB_BLOCK == 0 and S % FFN_TM == 0
    bf = lambda w: w.astype(BF16)
    h = x.reshape(S, D)

    h, (hg_in, hg_out, f2_in, f2_out) = _ffn_ln(
        h, bf(l0_ffn1_in), bf(l0_ffn1_out), l0_ln1_g, l0_ln1_b,
        cast_weights=(l0_hg_in, l0_hg_out, l0_ffn2_in, l0_ffn2_out))
    proj = _proj(h, hg_in, F32)
    mix = _hgrn_core(proj, lb_logits, l0_hg_norm_g)
    h = _out_ln(h, mix, hg_out, l0_ln2_g, l0_ln2_b)
    h, (f1_in, f1_out) = _ffn_ln(h, f2_in, f2_out, l0_ln3_g, l0_ln3_b,
                                 cast_weights=(l1_ffn1_in, l1_ffn1_out))

    h, (mb_in, mb_out, f2_in, f2_out) = _ffn_ln(
        h, f1_in, f1_out, l1_ln1_g, l1_ln1_b,
        cast_weights=(l1_mb_in, l1_mb_out, l1_ffn2_in, l1_ffn2_out))
    qkv = _proj(h, mb_in, BF16)
    mix = _moba_core(qkv)
    h = _out_ln(h, mix, mb_out, l1_ln2_g, l1_ln2_b)
    h, _ = _ffn_ln(h, f2_in, f2_out, l1_ln3_g, l1_ln3_b)
    return h.reshape(B, S, D)
```

```python
import functools

import jax
import jax.numpy as jnp
from jax import lax
from jax.experimental import pallas as pl
from jax.experimental.pallas import tpu as pltpu

F32 = jnp.float32
BF16 = jnp.bfloat16

DEPTH = 2
ALPHA = float((2 * DEPTH) ** 0.25)
LN_EPS = 1e-5
RMS_EPS = 1e-6
LOG2_E = 1.4426950408889634
HEAD_DIM = 128
SUBLANES = 8
BF16_SUBLANES = 16
LANES = 128
MB_BLOCK = 256
MB_TOPK = 3
MB_UNROLL = 4
MB_HEADS = 2
MB_MAX_FLOOR = -1e30

VMEM_LIMIT_BYTES = 56 * 1024 * 1024

FFN_TM = 512
FFN_TF = 256
PROJ_TM = 1024
PROJ_TN = 512
OUT_TM = 512
OUT_TK = 1024
HG_TB = 256
HG_CHUNK = 64
HG_HB = 8
HG_SAFE_LOG = 60.0


def _params(*sem):
    return pltpu.CompilerParams(dimension_semantics=sem, vmem_limit_bytes=VMEM_LIMIT_BYTES)


LN_ROWS = 128
ACC_TN = 1024


def _layer_norm_inplace(o_ref, g_ref, b_ref):
    g = g_ref[...]
    b = b_ref[...]

    def body(r, carry):
        rows = pl.ds(pl.multiple_of(r * LN_ROWS, LN_ROWS), LN_ROWS)
        y = o_ref[rows, :]
        mu = jnp.mean(y, axis=-1, keepdims=True)
        yc = y - mu
        var = jnp.mean(yc * yc, axis=-1, keepdims=True)
        o_ref[rows, :] = yc * lax.rsqrt(var + LN_EPS) * g + b
        return carry

    lax.fori_loop(0, o_ref.shape[0] // LN_ROWS, body, 0)


def _accumulate_dot(o_ref, a, w_ref):
    tn = min(ACC_TN, o_ref.shape[1])
    assert o_ref.shape[1] % tn == 0
    for c in range(o_ref.shape[1] // tn):
        cols = slice(c * tn, (c + 1) * tn)
        o_ref[:, cols] += jnp.dot(a, w_ref[:, cols], preferred_element_type=F32)


def _ffn_ln_kernel(n_cast, x_ref, wg_ref, wu_ref, wo_ref, g_ref, b_ref, *refs):
    cast_in = refs[:n_cast]
    o_ref = refs[n_cast]
    cast_out = refs[n_cast + 1: 2 * n_cast + 1]
    xb_ref = refs[2 * n_cast + 1]
    j = pl.program_id(1)

    @pl.when(j == 0)
    def _():
        x = x_ref[...]
        xb_ref[...] = x.astype(BF16)
        o_ref[...] = ALPHA * x

    xb = xb_ref[...]
    gate = jnp.dot(xb, wg_ref[...], preferred_element_type=F32)
    up = jnp.dot(xb, wu_ref[...], preferred_element_type=F32)
    h = (0.5 * gate * jax.nn.sigmoid(gate) * up).astype(BF16)
    _accumulate_dot(o_ref, h, wo_ref)

    for src, dst in zip(cast_in, cast_out):
        dst[...] = src[...].astype(BF16)

    @pl.when(j == pl.num_programs(1) - 1)
    def _():
        _layer_norm_inplace(o_ref, g_ref, b_ref)


def _cast_tile_spec(w, ni, nj):
    R, C = w.shape
    rows = R // ni
    assert rows * ni == R and rows % BF16_SUBLANES == 0
    width = next(c for c in (LANES, 2 * LANES, 4 * LANES, 8 * LANES) if C % c == 0 and C // c <= nj)
    ncols = C // width
    return pl.BlockSpec((rows, width), lambda i, j: (i, jnp.minimum(j, ncols - 1)))


def _ffn_ln(x, w_in, w_out, g, b, cast_weights=()):
    S, D = x.shape
    F = w_out.shape[0]
    nf = F // FFN_TF
    ni = S // FFN_TM
    cast_specs = [_cast_tile_spec(w, ni, nf) for w in cast_weights]
    outs = pl.pallas_call(
        functools.partial(_ffn_ln_kernel, len(cast_weights)),
        grid=(ni, nf),
        in_specs=[
            pl.BlockSpec((FFN_TM, D), lambda i, j: (i, 0)),
            pl.BlockSpec((D, FFN_TF), lambda i, j: (0, j)),
            pl.BlockSpec((D, FFN_TF), lambda i, j: (0, j + nf)),
            pl.BlockSpec((FFN_TF, D), lambda i, j: (j, 0)),
            pl.BlockSpec((1, D), lambda i, j: (0, 0)),
            pl.BlockSpec((1, D), lambda i, j: (0, 0)),
        ] + cast_specs,
        out_specs=[pl.BlockSpec((FFN_TM, D), lambda i, j: (i, 0))] + cast_specs,
        out_shape=[jax.ShapeDtypeStruct((S, D), F32)]
                  + [jax.ShapeDtypeStruct(w.shape, BF16) for w in cast_weights],
        scratch_shapes=[pltpu.VMEM((FFN_TM, D), BF16)],
        compiler_params=_params("parallel", "arbitrary"),
        name="ffn_ln",
    )(x, w_in, w_in, w_out, g.reshape(1, D), b.reshape(1, D), *cast_weights)
    return outs[0], tuple(outs[1:])


def _proj_kernel(x_ref, w_ref, o_ref, xb_ref):
    @pl.when(pl.program_id(1) == 0)
    def _():
        xb_ref[...] = x_ref[...].astype(BF16)

    o_ref[...] = jnp.dot(xb_ref[...], w_ref[...], preferred_element_type=F32).astype(o_ref.dtype)


def _proj(x, w, out_dtype):
    S, D = x.shape
    N = w.shape[1]
    return pl.pallas_call(
        _proj_kernel,
        grid=(S // PROJ_TM, N // PROJ_TN),
        in_specs=[
            pl.BlockSpec((PROJ_TM, D), lambda i, j: (i, 0)),
            pl.BlockSpec((D, PROJ_TN), lambda i, j: (0, j)),
        ],
        out_specs=pl.BlockSpec((PROJ_TM, PROJ_TN), lambda i, j: (i, j)),
        out_shape=jax.ShapeDtypeStruct((S, N), out_dtype),
        scratch_shapes=[pltpu.VMEM((PROJ_TM, D), BF16)],
        compiler_params=_params("parallel", "arbitrary"),
        name="proj",
    )(x, w)


def _out_ln_kernel(x_ref, a_ref, w_ref, g_ref, b_ref, o_ref):
    k = pl.program_id(1)

    @pl.when(k == 0)
    def _():
        o_ref[...] = ALPHA * x_ref[...]

    _accumulate_dot(o_ref, a_ref[...], w_ref)

    @pl.when(k == pl.num_programs(1) - 1)
    def _():
        _layer_norm_inplace(o_ref, g_ref, b_ref)


def _out_ln(x, a, w, g, b):
    S, D = x.shape
    K = a.shape[1]
    return pl.pallas_call(
        _out_ln_kernel,
        grid=(S // OUT_TM, K // OUT_TK),
        in_specs=[
            pl.BlockSpec((OUT_TM, D), lambda i, k: (i, 0)),
            pl.BlockSpec((OUT_TM, OUT_TK), lambda i, k: (i, k)),
            pl.BlockSpec((OUT_TK, D), lambda i, k: (k, 0)),
            pl.BlockSpec((1, D), lambda i, k: (0, 0)),
            pl.BlockSpec((1, D), lambda i, k: (0, 0)),
        ],
        out_specs=pl.BlockSpec((OUT_TM, D), lambda i, k: (i, 0)),
        out_shape=jax.ShapeDtypeStruct((S, D), F32),
        compiler_params=_params("parallel", "arbitrary"),
        name="out_ln",
    )(x, a, w, g.reshape(1, D), b.reshape(1, D))


def _nt_dot(a, b):
    return lax.dot_general(a, b, (((1,), (1,)), ((), ())), preferred_element_type=F32)


def _tn_dot(a, b):
    return lax.dot_general(a, b, (((0,), (0,)), ((), ())), preferred_element_type=F32)


def _hgrn_kernel(q_ref, f_ref, v_ref, g_ref, lb_ref, ng_ref, o_ref,
                 st_ref, cum_ref, qs_ref, oi_ref):
    TB, W = q_ref.shape
    C = HG_CHUNK
    nchunk = TB // C
    nhead = W // HEAD_DIM

    @pl.when(pl.program_id(1) == 0)
    def _():
        st_ref[...] = jnp.zeros_like(st_ref)

    lg = lb_ref[...]
    ex = jnp.exp(lg - jnp.max(lg, axis=0, keepdims=True))
    lb = ex[0:1, :] / jnp.sum(ex, axis=0, keepdims=True)
    qv = q_ref[...]
    q = qv * jax.nn.sigmoid(qv)
    forget = lb + (1.0 - lb) * jax.nn.sigmoid(f_ref[...])
    k = 1.0 - forget
    logf = jnp.log(forget)

    r = lax.broadcasted_iota(jnp.int32, (TB, TB), 0)
    c = lax.broadcasted_iota(jnp.int32, (TB, TB), 1)
    tri = jnp.where((c <= r) & (c // C == r // C), 1.0, 0.0).astype(BF16)
    hi = logf.astype(BF16)
    r1 = logf - hi.astype(F32)
    mid = r1.astype(BF16)
    lo = (r1 - mid.astype(F32)).astype(BF16)
    cum = (jnp.dot(tri, hi, preferred_element_type=F32)
           + jnp.dot(tri, mid, preferred_element_type=F32)
           + jnp.dot(tri, lo, preferred_element_type=F32))

    bound = jnp.zeros((1, W), F32)
    for ci in range(nchunk):
        mrow = cum[ci * C + C // 2 - 1: ci * C + C // 2, :]
        lrow = cum[ci * C + C - 1: ci * C + C, :]
        bound = jnp.maximum(bound, jnp.maximum(-mrow, mrow - lrow))
    safe = jnp.max(bound) <= HG_SAFE_LOG

    tr = lax.broadcasted_iota(jnp.int32, (C, C), 0)
    ts = lax.broadcasted_iota(jnp.int32, (C, C), 1)
    causal = ts <= tr

    gv = g_ref[...]
    og = ng_ref[...] * (gv * jax.nn.sigmoid(gv))

    def factorised_intra(rows, lanes):
        cu = cum[rows, lanes]
        m = cu[C // 2 - 1: C // 2, :]
        qd = (q[rows, lanes] * jnp.exp(cu - m)).astype(BF16)
        kd = (k[rows, lanes] * jnp.exp(m - cu)).astype(BF16)
        a = jnp.where(causal, _nt_dot(qd, kd), 0.0).astype(BF16)
        return jnp.dot(a, v_ref[rows, lanes].astype(BF16), preferred_element_type=F32)

    def stored_intra(rows, lanes):
        return oi_ref[rows, lanes]

    def recurrence(intra):
        for h in range(nhead):
            lanes = slice(h * HEAD_DIM, (h + 1) * HEAD_DIM)
            st = st_ref[h]
            for ci in range(nchunk):
                rows = slice(ci * C, (ci + 1) * C)
                cu = cum[rows, lanes]
                last = cu[C - 1: C, :]
                qe = (q[rows, lanes] * jnp.exp(cu)).astype(BF16)
                o = intra(rows, lanes) + _nt_dot(qe, st.astype(BF16))
                ke = (k[rows, lanes] * jnp.exp(last - cu)).astype(BF16)
                st = st * jnp.exp(last) + _tn_dot(v_ref[rows, lanes].astype(BF16), ke)
                o = o * lax.rsqrt(jnp.mean(o * o, axis=-1, keepdims=True) + RMS_EPS)
                o_ref[rows, lanes] = (o * og[rows, lanes]).astype(o_ref.dtype)
            st_ref[h] = st

    @pl.when(safe)
    def _():
        recurrence(factorised_intra)

    @pl.when(jnp.logical_not(safe))
    def _():
        cum_ref[...] = cum
        qs_ref[...] = q
        sidx = lax.broadcasted_iota(jnp.int32, (C, W), 0)
        for ci in range(nchunk):
            rows = slice(ci * C, (ci + 1) * C)
            cu = cum[rows, :]
            kc = k[rows, :]
            vc = v_ref[rows, :]

            def rows_body(t8, carry, ci=ci, cu=cu, kc=kc, vc=vc):
                base = pl.multiple_of(ci * C + t8 * SUBLANES, SUBLANES)
                c8 = cum_ref[pl.ds(base, SUBLANES), :]
                q8 = qs_ref[pl.ds(base, SUBLANES), :]
                out_rows = []
                for r in range(SUBLANES):
                    t = t8 * SUBLANES + r
                    dec = jnp.exp(jnp.where(sidx <= t, c8[r:r + 1, :] - cu, -jnp.inf))
                    prod = q8[r:r + 1, :] * dec * kc
                    heads = []
                    for h in range(nhead):
                        lanes = slice(h * HEAD_DIM, (h + 1) * HEAD_DIM)
                        sc = jnp.sum(prod[:, lanes], axis=1, keepdims=True)
                        heads.append(jnp.sum(sc * vc[:, lanes], axis=0, keepdims=True))
                    out_rows.append(jnp.concatenate(heads, axis=1))
                oi_ref[pl.ds(base, SUBLANES), :] = jnp.concatenate(out_rows, axis=0)
                return carry

            lax.fori_loop(0, C // SUBLANES, rows_body, 0)
        recurrence(stored_intra)


def _hgrn_core(proj, lb_logits, norm_g):
    S, D4 = proj.shape
    D = D4 // 4
    W = HG_HB * HEAD_DIM
    ng = D // W
    return pl.pallas_call(
        _hgrn_kernel,
        grid=(ng, S // HG_TB),
        in_specs=[
            pl.BlockSpec((HG_TB, W), lambda hg, t: (t, hg)),
            pl.BlockSpec((HG_TB, W), lambda hg, t: (t, hg + ng)),
            pl.BlockSpec((HG_TB, W), lambda hg, t: (t, hg + 2 * ng)),
            pl.BlockSpec((HG_TB, W), lambda hg, t: (t, hg + 3 * ng)),
            pl.BlockSpec((lb_logits.shape[0], W), lambda hg, t: (0, hg)),
            pl.BlockSpec((1, W), lambda hg, t: (0, hg)),
        ],
        out_specs=pl.BlockSpec((HG_TB, W), lambda hg, t: (t, hg)),
        out_shape=jax.ShapeDtypeStruct((S, D), BF16),
        scratch_shapes=[
            pltpu.VMEM((HG_HB, HEAD_DIM, HEAD_DIM), F32),
            pltpu.VMEM((HG_TB, W), F32),
            pltpu.VMEM((HG_TB, W), F32),
            pltpu.VMEM((HG_TB, W), F32),
        ],
        compiler_params=_params("parallel", "arbitrary"),
        name="hgrn_core",
    )(proj, proj, proj, proj, lb_logits.astype(F32), norm_g.reshape(1, D))


def _moba_kernel(q_ref, k_ref, v_ref, o_ref, km_ref, sel_ref, s_ref, vt_ref):
    L = MB_BLOCK
    S = k_ref.shape[0]
    nb = S // L
    c2 = HEAD_DIM ** -0.5 * LOG2_E
    last_group = nb // MB_UNROLL - 1

    km_ref[...] = jnp.zeros_like(km_ref)

    def mean_body(n, carry):
        rows = pl.ds(pl.multiple_of(n * L, L), L)
        for h in range(MB_HEADS):
            lanes = slice(h * HEAD_DIM, (h + 1) * HEAD_DIM)
            km_ref[h, pl.ds(n + 1, 1), :] = jnp.mean(k_ref[rows, lanes].astype(F32), axis=0, keepdims=True)
            vt_ref[h, n] = v_ref[rows, lanes].astype(F32).T.astype(BF16)
        return carry
    lax.fori_loop(0, nb, mean_body, 0)

    def tile_body(i, carry):
        _moba_tile(i, q_ref, k_ref, vt_ref, o_ref, km_ref, sel_ref, s_ref, c2, last_group)
        return carry
    lax.fori_loop(0, nb, tile_body, 0)


def _moba_tile(i, q_ref, k_ref, vt_ref, o_ref, km_ref, sel_ref, s_ref, c2, last_group):
    L = MB_BLOCK
    nrow = sel_ref.shape[1]
    heads = range(MB_HEADS)
    lanes = [slice(h * HEAD_DIM, (h + 1) * HEAD_DIM) for h in heads]
    own = pl.ds(pl.multiple_of(i * L, L), L)
    q_raw = [q_ref[own, lanes[h]] for h in heads]
    q = [(x.astype(F32) * c2).astype(BF16) for x in q_raw]

    ridx = lax.broadcasted_iota(jnp.int32, (nrow, L), 0)
    for h in heads:
        gate = _nt_dot(km_ref[h].astype(BF16), q_raw[h])
        gate = jnp.where((ridx >= 1) & (ridx <= i), gate, -jnp.inf)
        sel = jnp.where(ridx == 0, 1.0, 0.0)
        for _ in range(MB_TOPK):
            mx = jnp.max(gate, axis=0, keepdims=True)
            first = jnp.min(jnp.where(gate == mx, ridx, nrow), axis=0, keepdims=True)
            pick = (ridx == first) & (mx > -jnp.inf)
            sel = jnp.where(pick, 1.0, sel)
            gate = jnp.where(pick, -jnp.inf, gate)
        sel_ref[h] = sel

    kpos = lax.broadcasted_iota(jnp.int32, (L, L), 0)
    qpos = lax.broadcasted_iota(jnp.int32, (L, L), 1)

    def key_block(r):
        return jnp.where(r == 0, i, r - 1)

    def score_member(h, r, slot, u, colmax, causal):
        n = key_block(r)
        s = _nt_dot(k_ref[pl.ds(pl.multiple_of(n * L, L), L), lanes[h]], q[h])
        if causal:
            s = jnp.where(kpos <= qpos, s, -jnp.inf)
        s_ref[h, slot, u] = s
        on = sel_ref[h, pl.ds(r, 1), :] > 0.0
        return jnp.maximum(colmax, jnp.where(on, jnp.max(s, axis=0, keepdims=True), -jnp.inf))

    def score_group(g, slot, first_group=False):
        colmax = [jnp.full((1, L), -jnp.inf, F32) for _ in heads]
        for u in range(MB_UNROLL):
            r = g * MB_UNROLL + u
            for h in heads:
                colmax[h] = score_member(h, r, slot, u, colmax[h], causal=first_group and u == 0)
        return colmax

    def attend_group(g, slot, colmax, m, l, acc):
        m, l, acc = list(m), list(l), list(acc)
        for h in heads:
            m_new = jnp.maximum(m[h], colmax[h])
            alpha = jnp.exp2(m[h] - m_new)
            lh = alpha * l[h]
            ah = alpha * acc[h]
            for u in range(MB_UNROLL):
                r = g * MB_UNROLL + u
                on = sel_ref[h, pl.ds(r, 1), :] > 0.0
                p = jnp.exp2(s_ref[h, slot, u] - jnp.where(on, m_new, jnp.inf))
                lh = lh + jnp.sum(p, axis=0, keepdims=True)
                ah = ah + jnp.dot(vt_ref[h, key_block(r)], p.astype(BF16), preferred_element_type=F32)
            m[h], l[h], acc[h] = m_new, lh, ah
        return m, l, acc

    def pair_body(pi, carry):
        m, l, acc, cm_even = carry
        g = 2 * pi
        cm_odd = score_group(g + 1, 1)
        m, l, acc = attend_group(g, 0, cm_even, m, l, acc)
        cm_even = score_group(jnp.minimum(g + 2, last_group), 0)
        m, l, acc = attend_group(g + 1, 1, cm_odd, m, l, acc)
        return m, l, acc, cm_even

    m0 = [jnp.full((1, L), MB_MAX_FLOOR, F32) for _ in heads]
    l0 = [jnp.zeros((1, L), F32) for _ in heads]
    acc0 = [jnp.zeros((HEAD_DIM, L), F32) for _ in heads]
    ngroups = i // MB_UNROLL + 1
    m, l, acc, cm_even = lax.fori_loop(0, ngroups // 2, pair_body,
                                       (m0, l0, acc0, score_group(0, 0, first_group=True)))
    m, l, acc = lax.cond(ngroups % 2 == 1,
                         lambda: attend_group(ngroups - 1, 0, cm_even, m, l, acc),
                         lambda: (list(m), list(l), list(acc)))
    for h in heads:
        o_ref[own, lanes[h]] = (acc[h] / l[h]).T.astype(o_ref.dtype)


def _moba_core(qkv):
    S, D3 = qkv.shape
    D = D3 // 3
    L = MB_BLOCK
    nb = S // L
    W = MB_HEADS * HEAD_DIM
    ng = D // W
    nrow = nb + SUBLANES
    assert nb % (2 * MB_UNROLL) == 0 and ng * W == D
    return pl.pallas_call(
        _moba_kernel,
        grid=(ng,),
        in_specs=[
            pl.BlockSpec((S, W), lambda hg: (0, hg)),
            pl.BlockSpec((S, W), lambda hg: (0, ng + hg)),
            pl.BlockSpec((S, W), lambda hg: (0, 2 * ng + hg)),
        ],
        out_specs=pl.BlockSpec((S, W), lambda hg: (0, hg)),
        out_shape=jax.ShapeDtypeStruct((S, D), BF16),
        scratch_shapes=[
            pltpu.VMEM((MB_HEADS, nrow, HEAD_DIM), F32),
            pltpu.VMEM((MB_HEADS, nrow, L), F32),
            pltpu.VMEM((MB_HEADS, 2, MB_UNROLL, L, L), F32),
            pltpu.VMEM((MB_HEADS, nb, HEAD_DIM, L), BF16),
        ],
        compiler_params=_params("parallel"),
        name="moba_core",
    )(qkv, qkv, qkv)


def kernel(x, lb_logits, l0_ffn1_in, l0_ffn1_out, l0_ln1_g, l0_ln1_b, l0_hg_in, l0_hg_norm_g, l0_hg_out, l0_ln2_g, l0_ln2_b, l0_ffn2_in, l0_ffn2_out, l0_ln3_g, l0_ln3_b, l1_ffn1_in, l1_ffn1_out, l1_ln1_g, l1_ln1_b, l1_mb_in, l1_mb_out, l1_ln2_g, l1_ln2_b, l1_ffn2_in, l1_ffn2_out, l1_ln3_g, l1_ln3_b):
    B, S, D = x.shape
    assert B == 1 and S % MB_BLOCK == 0 and S % FFN_TM == 0
    bf = lambda w: w.astype(BF16)
    h = x.reshape(S, D)

    h, (hg_in, hg_out, f2_in, f2_out) = _ffn_ln(
        h, bf(l0_ffn1_in), bf(l0_ffn1_out), l0_ln1_g, l0_ln1_b,
        cast_weights=(l0_hg_in, l0_hg_out, l0_ffn2_in, l0_ffn2_out))
    proj = _proj(h, hg_in, F32)
    mix = _hgrn_core(proj, lb_logits, l0_hg_norm_g)
    h = _out_ln(h, mix, hg_out, l0_ln2_g, l0_ln2_b)
    h, (f1_in, f1_out) = _ffn_ln(h, f2_in, f2_out, l0_ln3_g, l0_ln3_b,
                                 cast_weights=(l1_ffn1_in, l1_ffn1_out))

    h, (mb_in, mb_out, f2_in, f2_out) = _ffn_ln(
        h, f1_in, f1_out, l1_ln1_g, l1_ln1_b,
        cast_weights=(l1_mb_in, l1_mb_out, l1_ffn2_in, l1_ffn2_out))
    qkv = _proj(h, mb_in, BF16)
    mix = _moba_core(qkv)
    h = _out_ln(h, mix, mb_out, l1_ln2_g, l1_ln2_b)
    h, _ = _ffn_ln(h, f2_in, f2_out, l1_ln3_g, l1_ln3_b)
    return h.reshape(B, S, D)
```

```python
import functools

import jax
import jax.numpy as jnp
from jax import lax
from jax.experimental import pallas as pl
from jax.experimental.pallas import tpu as pltpu

F32 = jnp.float32
BF16 = jnp.bfloat16

DEPTH = 2
ALPHA = float((2 * DEPTH) ** 0.25)
LN_EPS = 1e-5
RMS_EPS = 1e-6
LOG2_E = 1.4426950408889634
HEAD_DIM = 128
SUBLANES = 8
BF16_SUBLANES = 16
LANES = 128
MB_BLOCK = 256
MB_TOPK = 3
MB_UNROLL = 4
MB_HEADS = 2
MB_MAX_FLOOR = -1e30

VMEM_LIMIT_BYTES = 56 * 1024 * 1024

FFN_TM = 512
FFN_TF = 256
PROJ_TM = 1024
PROJ_TN = 512
OUT_TM = 512
OUT_TK = 1024
HG_TB = 256
HG_CHUNK = 64
HG_HB = 8
HG_SAFE_LOG = 60.0


def _params(*sem):
    return pltpu.CompilerParams(dimension_semantics=sem, vmem_limit_bytes=VMEM_LIMIT_BYTES)


LN_ROWS = 128
ACC_TN = 1024


def _layer_norm_inplace(o_ref, g_ref, b_ref):
    g = g_ref[...]
    b = b_ref[...]

    def body(r, carry):
        rows = pl.ds(pl.multiple_of(r * LN_ROWS, LN_ROWS), LN_ROWS)
        y = o_ref[rows, :]
        mu = jnp.mean(y, axis=-1, keepdims=True)
        yc = y - mu
        var = jnp.mean(yc * yc, axis=-1, keepdims=True)
        o_ref[rows, :] = yc * lax.rsqrt(var + LN_EPS) * g + b
        return carry

    lax.fori_loop(0, o_ref.shape[0] // LN_ROWS, body, 0)


def _accumulate_dot(o_ref, a, w_ref):
    tn = min(ACC_TN, o_ref.shape[1])
    assert o_ref.shape[1] % tn == 0
    for c in range(o_ref.shape[1] // tn):
        cols = slice(c * tn, (c + 1) * tn)
        o_ref[:, cols] += jnp.dot(a, w_ref[:, cols], preferred_element_type=F32)


def _ffn_ln_kernel(n_cast, x_ref, wg_ref, wu_ref, wo_ref, g_ref, b_ref, *refs):
    cast_in = refs[:n_cast]
    o_ref = refs[n_cast]
    cast_out = refs[n_cast + 1: 2 * n_cast + 1]
    xb_ref = refs[2 * n_cast + 1]
    j = pl.program_id(1)

    @pl.when(j == 0)
    def _():
        x = x_ref[...]
        xb_ref[...] = x.astype(BF16)
        o_ref[...] = ALPHA * x

    xb = xb_ref[...]
    gate = jnp.dot(xb, wg_ref[...], preferred_element_type=F32)
    up = jnp.dot(xb, wu_ref[...], preferred_element_type=F32)
    h = (0.5 * gate * jax.nn.sigmoid(gate) * up).astype(BF16)
    _accumulate_dot(o_ref, h, wo_ref)

    for src, dst in zip(cast_in, cast_out):
        dst[...] = src[...].astype(BF16)

    @pl.when(j == pl.num_programs(1) - 1)
    def _():
        _layer_norm_inplace(o_ref, g_ref, b_ref)


def _cast_tile_spec(w, ni, nj):
    R, C = w.shape
    rows = R // ni
    assert rows * ni == R and rows % BF16_SUBLANES == 0
    width = next(c for c in (LANES, 2 * LANES, 4 * LANES, 8 * LANES) if C % c == 0 and C // c <= nj)
    ncols = C // width
    return pl.BlockSpec((rows, width), lambda i, j: (i, jnp.minimum(j, ncols - 1)))


def _ffn_ln(x, w_in, w_out, g, b, cast_weights=()):
    S, D = x.shape
    F = w_out.shape[0]
    nf = F // FFN_TF
    ni = S // FFN_TM
    cast_specs = [_cast_tile_spec(w, ni, nf) for w in cast_weights]
    outs = pl.pallas_call(
        functools.partial(_ffn_ln_kernel, len(cast_weights)),
        grid=(ni, nf),
        in_specs=[
            pl.BlockSpec((FFN_TM, D), lambda i, j: (i, 0)),
            pl.BlockSpec((D, FFN_TF), lambda i, j: (0, j)),
            pl.BlockSpec((D, FFN_TF), lambda i, j: (0, j + nf)),
            pl.BlockSpec((FFN_TF, D), lambda i, j: (j, 0)),
            pl.BlockSpec((1, D), lambda i, j: (0, 0)),
            pl.BlockSpec((1, D), lambda i, j: (0, 0)),
        ] + cast_specs,
        out_specs=[pl.BlockSpec((FFN_TM, D), lambda i, j: (i, 0))] + cast_specs,
        out_shape=[jax.ShapeDtypeStruct((S, D), F32)]
                  + [jax.ShapeDtypeStruct(w.shape, BF16) for w in cast_weights],
        scratch_shapes=[pltpu.VMEM((FFN_TM, D), BF16)],
        compiler_params=_params("parallel", "arbitrary"),
        name="ffn_ln",
    )(x, w_in, w_in, w_out, g.reshape(1, D), b.reshape(1, D), *cast_weights)
    return outs[0], tuple(outs[1:])


def _proj_kernel(x_ref, w_ref, o_ref, xb_ref):
    @pl.when(pl.program_id(1) == 0)
    def _():
        xb_ref[...] = x_ref[...].astype(BF16)

    o_ref[...] = jnp.dot(xb_ref[...], w_ref[...], preferred_element_type=F32).astype(o_ref.dtype)


def _proj(x, w, out_dtype):
    S, D = x.shape
    N = w.shape[1]
    return pl.pallas_call(
        _proj_kernel,
        grid=(S // PROJ_TM, N // PROJ_TN),
        in_specs=[
            pl.BlockSpec((PROJ_TM, D), lambda i, j: (i, 0)),
            pl.BlockSpec((D, PROJ_TN), lambda i, j: (0, j)),
        ],
        out_specs=pl.BlockSpec((PROJ_TM, PROJ_TN), lambda i, j: (i, j)),
        out_shape=jax.ShapeDtypeStruct((S, N), out_dtype),
        scratch_shapes=[pltpu.VMEM((PROJ_TM, D), BF16)],
        compiler_params=_params("parallel", "arbitrary"),
        name="proj",
    )(x, w)


def _out_ln_kernel(x_ref, a_ref, w_ref, g_ref, b_ref, o_ref):
    k = pl.program_id(1)

    @pl.when(k == 0)
    def _():
        o_ref[...] = ALPHA * x_ref[...]

    _accumulate_dot(o_ref, a_ref[...], w_ref)

    @pl.when(k == pl.num_programs(1) - 1)
    def _():
        _layer_norm_inplace(o_ref, g_ref, b_ref)


def _out_ln(x, a, w, g, b):
    S, D = x.shape
    K = a.shape[1]
    return pl.pallas_call(
        _out_ln_kernel,
        grid=(S // OUT_TM, K // OUT_TK),
        in_specs=[
            pl.BlockSpec((OUT_TM, D), lambda i, k: (i, 0)),
            pl.BlockSpec((OUT_TM, OUT_TK), lambda i, k: (i, k)),
            pl.BlockSpec((OUT_TK, D), lambda i, k: (k, 0)),
            pl.BlockSpec((1, D), lambda i, k: (0, 0)),
            pl.BlockSpec((1, D), lambda i, k: (0, 0)),
        ],
        out_specs=pl.BlockSpec((OUT_TM, D), lambda i, k: (i, 0)),
        out_shape=jax.ShapeDtypeStruct((S, D), F32),
        compiler_params=_params("parallel", "arbitrary"),
        name="out_ln",
    )(x, a, w, g.reshape(1, D), b.reshape(1, D))


def _nt_dot(a, b):
    return lax.dot_general(a, b, (((1,), (1,)), ((), ())), preferred_element_type=F32)


def _tn_dot(a, b):
    return lax.dot_general(a, b, (((0,), (0,)), ((), ())), preferred_element_type=F32)


def _hgrn_kernel(q_ref, f_ref, v_ref, g_ref, lb_ref, ng_ref, o_ref,
                 st_ref, cum_ref, qs_ref, oi_ref):
    TB, W = q_ref.shape
    C = HG_CHUNK
    nchunk = TB // C
    nhead = W // HEAD_DIM

    @pl.when(pl.program_id(1) == 0)
    def _():
        st_ref[...] = jnp.zeros_like(st_ref)

    lg = lb_ref[...]
    ex = jnp.exp(lg - jnp.max(lg, axis=0, keepdims=True))
    lb = ex[0:1, :] / jnp.sum(ex, axis=0, keepdims=True)
    qv = q_ref[...]
    q = qv * jax.nn.sigmoid(qv)
    forget = lb + (1.0 - lb) * jax.nn.sigmoid(f_ref[...])
    k = 1.0 - forget
    logf = jnp.log(forget)

    r = lax.broadcasted_iota(jnp.int32, (TB, TB), 0)
    c = lax.broadcasted_iota(jnp.int32, (TB, TB), 1)
    tri = jnp.where((c <= r) & (c // C == r // C), 1.0, 0.0).astype(BF16)
    hi = logf.astype(BF16)
    r1 = logf - hi.astype(F32)
    mid = r1.astype(BF16)
    lo = (r1 - mid.astype(F32)).astype(BF16)
    cum = (jnp.dot(tri, hi, preferred_element_type=F32)
           + jnp.dot(tri, mid, preferred_element_type=F32)
           + jnp.dot(tri, lo, preferred_element_type=F32))

    bound = jnp.zeros((1, W), F32)
    for ci in range(nchunk):
        mrow = cum[ci * C + C // 2 - 1: ci * C + C // 2, :]
        lrow = cum[ci * C + C - 1: ci * C + C, :]
        bound = jnp.maximum(bound, jnp.maximum(-mrow, mrow - lrow))
    safe = jnp.max(bound) <= HG_SAFE_LOG

    tr = lax.broadcasted_iota(jnp.int32, (C, C), 0)
    ts = lax.broadcasted_iota(jnp.int32, (C, C), 1)
    causal = ts <= tr

    gv = g_ref[...]
    og = ng_ref[...] * (gv * jax.nn.sigmoid(gv))

    def factorised_intra(rows, lanes):
        cu = cum[rows, lanes]
        m = cu[C // 2 - 1: C // 2, :]
        qd = (q[rows, lanes] * jnp.exp(cu - m)).astype(BF16)
        kd = (k[rows, lanes] * jnp.exp(m - cu)).astype(BF16)
        a = jnp.where(causal, _nt_dot(qd, kd), 0.0).astype(BF16)
        return jnp.dot(a, v_ref[rows, lanes].astype(BF16), preferred_element_type=F32)

    def stored_intra(rows, lanes):
        return oi_ref[rows, lanes]

    def recurrence(intra):
        for h in range(nhead):
            lanes = slice(h * HEAD_DIM, (h + 1) * HEAD_DIM)
            st = st_ref[h]
            for ci in range(nchunk):
                rows = slice(ci * C, (ci + 1) * C)
                cu = cum[rows, lanes]
                last = cu[C - 1: C, :]
                qe = (q[rows, lanes] * jnp.exp(cu)).astype(BF16)
                o = intra(rows, lanes) + _nt_dot(qe, st.astype(BF16))
                ke = (k[rows, lanes] * jnp.exp(last - cu)).astype(BF16)
                st = st * jnp.exp(last) + _tn_dot(v_ref[rows, lanes].astype(BF16), ke)
                o = o * lax.rsqrt(jnp.mean(o * o, axis=-1, keepdims=True) + RMS_EPS)
                o_ref[rows, lanes] = (o * og[rows, lanes]).astype(o_ref.dtype)
            st_ref[h] = st

    @pl.when(safe)
    def _():
        recurrence(factorised_intra)

    @pl.when(jnp.logical_not(safe))
    def _():
        cum_ref[...] = cum
        qs_ref[...] = q
        sidx = lax.broadcasted_iota(jnp.int32, (C, W), 0)
        for ci in range(nchunk):
            rows = slice(ci * C, (ci + 1) * C)
            cu = cum[rows, :]
            kc = k[rows, :]
            vc = v_ref[rows, :]

            def rows_body(t8, carry, ci=ci, cu=cu, kc=kc, vc=vc):
                base = pl.multiple_of(ci * C + t8 * SUBLANES, SUBLANES)
                c8 = cum_ref[pl.ds(base, SUBLANES), :]
                q8 = qs_ref[pl.ds(base, SUBLANES), :]
                out_rows = []
                for r in range(SUBLANES):
                    t = t8 * SUBLANES + r
                    dec = jnp.exp(jnp.where(sidx <= t, c8[r:r + 1, :] - cu, -jnp.inf))
                    prod = q8[r:r + 1, :] * dec * kc
                    heads = []
                    for h in range(nhead):
                        lanes = slice(h * HEAD_DIM, (h + 1) * HEAD_DIM)
                        sc = jnp.sum(prod[:, lanes], axis=1, keepdims=True)
                        heads.append(jnp.sum(sc * vc[:, lanes], axis=0, keepdims=True))
                    out_rows.append(jnp.concatenate(heads, axis=1))
                oi_ref[pl.ds(base, SUBLANES), :] = jnp.concatenate(out_rows, axis=0)
                return carry

            lax.fori_loop(0, C // SUBLANES, rows_body, 0)
        recurrence(stored_intra)


def _hgrn_core(proj, lb_logits, norm_g):
    S, D4 = proj.shape
    D = D4 // 4
    W = HG_HB * HEAD_DIM
    ng = D // W
    return pl.pallas_call(
        _hgrn_kernel,
        grid=(ng, S // HG_TB),
        in_specs=[
            pl.BlockSpec((HG_TB, W), lambda hg, t: (t, hg)),
            pl.BlockSpec((HG_TB, W), lambda hg, t: (t, hg + ng)),
            pl.BlockSpec((HG_TB, W), lambda hg, t: (t, hg + 2 * ng)),
            pl.BlockSpec((HG_TB, W), lambda hg, t: (t, hg + 3 * ng)),
            pl.BlockSpec((lb_logits.shape[0], W), lambda hg, t: (0, hg)),
            pl.BlockSpec((1, W), lambda hg, t: (0, hg)),
        ],
        out_specs=pl.BlockSpec((HG_TB, W), lambda hg, t: (t, hg)),
        out_shape=jax.ShapeDtypeStruct((S, D), BF16),
        scratch_shapes=[
            pltpu.VMEM((HG_HB, HEAD_DIM, HEAD_DIM), F32),
            pltpu.VMEM((HG_TB, W), F32),
            pltpu.VMEM((HG_TB, W), F32),
            pltpu.VMEM((HG_TB, W), F32),
        ],
        compiler_params=_params("parallel", "arbitrary"),
        name="hgrn_core",
    )(proj, proj, proj, proj, lb_logits.astype(F32), norm_g.reshape(1, D))


def _moba_kernel(q_ref, k_ref, v_ref, o_ref, km_ref, sel_ref, s_ref, vt_ref):
    L = MB_BLOCK
    S = k_ref.shape[0]
    nb = S // L
    c2 = HEAD_DIM ** -0.5 * LOG2_E
    last_group = nb // MB_UNROLL - 1

    km_ref[...] = jnp.zeros_like(km_ref)

    def mean_body(n, carry):
        rows = pl.ds(pl.multiple_of(n * L, L), L)
        for h in range(MB_HEADS):
            lanes = slice(h * HEAD_DIM, (h + 1) * HEAD_DIM)
            km_ref[h, pl.ds(n + 1, 1), :] = jnp.mean(k_ref[rows, lanes].astype(F32), axis=0, keepdims=True)
            vt_ref[h, n] = v_ref[rows, lanes].astype(F32).T.astype(BF16)
        return carry
    lax.fori_loop(0, nb, mean_body, 0)

    def tile_body(i, carry):
        _moba_tile(i, q_ref, k_ref, vt_ref, o_ref, km_ref, sel_ref, s_ref, c2, last_group)
        return carry
    lax.fori_loop(0, nb, tile_body, 0)


def _moba_tile(i, q_ref, k_ref, vt_ref, o_ref, km_ref, sel_ref, s_ref, c2, last_group):
    L = MB_BLOCK
    nrow = sel_ref.shape[1]
    heads = range(MB_HEADS)
    lanes = [slice(h * HEAD_DIM, (h + 1) * HEAD_DIM) for h in heads]
    own = pl.ds(pl.multiple_of(i * L, L), L)
    qt32 = [q_ref[own, lanes[h]].astype(F32).T for h in heads]
    qt_raw = [x.astype(BF16) for x in qt32]
    qt = [(x * c2).astype(BF16) for x in qt32]

    ridx = lax.broadcasted_iota(jnp.int32, (nrow, L), 0)
    for h in heads:
        gate = jnp.dot(km_ref[h].astype(BF16), qt_raw[h], preferred_element_type=F32)
        gate = jnp.where((ridx >= 1) & (ridx <= i), gate, -jnp.inf)
        sel = jnp.where(ridx == 0, 1.0, 0.0)
        for _ in range(MB_TOPK):
            mx = jnp.max(gate, axis=0, keepdims=True)
            first = jnp.min(jnp.where(gate == mx, ridx, nrow), axis=0, keepdims=True)
            pick = (ridx == first) & (mx > -jnp.inf)
            sel = jnp.where(pick, 1.0, sel)
            gate = jnp.where(pick, -jnp.inf, gate)
        sel_ref[h] = sel

    kpos = lax.broadcasted_iota(jnp.int32, (L, L), 0)
    qpos = lax.broadcasted_iota(jnp.int32, (L, L), 1)

    def key_block(r):
        return jnp.where(r == 0, i, r - 1)

    def score_member(h, r, slot, u, colmax, causal):
        n = key_block(r)
        s = jnp.dot(k_ref[pl.ds(pl.multiple_of(n * L, L), L), lanes[h]], qt[h],
                    preferred_element_type=F32)
        if causal:
            s = jnp.where(kpos <= qpos, s, -jnp.inf)
        s_ref[h, slot, u] = s
        on = sel_ref[h, pl.ds(r, 1), :] > 0.0
        return jnp.maximum(colmax, jnp.where(on, jnp.max(s, axis=0, keepdims=True), -jnp.inf))

    def score_group(g, slot, first_group=False):
        colmax = [jnp.full((1, L), -jnp.inf, F32) for _ in heads]
        for u in range(MB_UNROLL):
            r = g * MB_UNROLL + u
            for h in heads:
                colmax[h] = score_member(h, r, slot, u, colmax[h], causal=first_group and u == 0)
        return colmax

    def attend_group(g, slot, colmax, m, l, acc):
        m, l, acc = list(m), list(l), list(acc)
        for h in heads:
            m_new = jnp.maximum(m[h], colmax[h])
            alpha = jnp.exp2(m[h] - m_new)
            lh = alpha * l[h]
            ah = alpha * acc[h]
            for u in range(MB_UNROLL):
                r = g * MB_UNROLL + u
                on = sel_ref[h, pl.ds(r, 1), :] > 0.0
                p = jnp.exp2(s_ref[h, slot, u] - jnp.where(on, m_new, jnp.inf))
                lh = lh + jnp.sum(p, axis=0, keepdims=True)
                ah = ah + jnp.dot(vt_ref[h, key_block(r)], p.astype(BF16), preferred_element_type=F32)
            m[h], l[h], acc[h] = m_new, lh, ah
        return m, l, acc

    def pair_body(pi, carry):
        m, l, acc, cm_even = carry
        g = 2 * pi
        cm_odd = score_group(g + 1, 1)
        m, l, acc = attend_group(g, 0, cm_even, m, l, acc)
        cm_even = score_group(jnp.minimum(g + 2, last_group), 0)
        m, l, acc = attend_group(g + 1, 1, cm_odd, m, l, acc)
        return m, l, acc, cm_even

    m0 = [jnp.full((1, L), MB_MAX_FLOOR, F32) for _ in heads]
    l0 = [jnp.zeros((1, L), F32) for _ in heads]
    acc0 = [jnp.zeros((HEAD_DIM, L), F32) for _ in heads]
    ngroups = i // MB_UNROLL + 1
    m, l, acc, cm_even = lax.fori_loop(0, ngroups // 2, pair_body,
                                       (m0, l0, acc0, score_group(0, 0, first_group=True)))
    m, l, acc = lax.cond(ngroups % 2 == 1,
                         lambda: attend_group(ngroups - 1, 0, cm_even, m, l, acc),
                         lambda: (list(m), list(l), list(acc)))
    for h in heads:
        o_ref[own, lanes[h]] = (acc[h] / l[h]).T.astype(o_ref.dtype)


def _moba_core(qkv):
    S, D3 = qkv.shape
    D = D3 // 3
    L = MB_BLOCK
    nb = S // L
    W = MB_HEADS * HEAD_DIM
    ng = D // W
    nrow = nb + SUBLANES
    assert nb % (2 * MB_UNROLL) == 0 and ng * W == D
    return pl.pallas_call(
        _moba_kernel,
        grid=(ng,),
        in_specs=[
            pl.BlockSpec((S, W), lambda hg: (0, hg)),
            pl.BlockSpec((S, W), lambda hg: (0, ng + hg)),
            pl.BlockSpec((S, W), lambda hg: (0, 2 * ng + hg)),
        ],
        out_specs=pl.BlockSpec((S, W), lambda hg: (0, hg)),
        out_shape=jax.ShapeDtypeStruct((S, D), BF16),
        scratch_shapes=[
            pltpu.VMEM((MB_HEADS, nrow, HEAD_DIM), F32),
            pltpu.VMEM((MB_HEADS, nrow, L), F32),
            pltpu.VMEM((MB_HEADS, 2, MB_UNROLL, L, L), F32),
            pltpu.VMEM((MB_HEADS, nb, HEAD_DIM, L), BF16),
        ],
        compiler_params=_params("parallel"),
        name="moba_core",
    )(qkv, qkv, qkv)


def kernel(x, lb_logits, l0_ffn1_in, l0_ffn1_out, l0_ln1_g, l0_ln1_b, l0_hg_in, l0_hg_norm_g, l0_hg_out, l0_ln2_g, l0_ln2_b, l0_ffn2_in, l0_ffn2_out, l0_ln3_g, l0_ln3_b, l1_ffn1_in, l1_ffn1_out, l1_ln1_g, l1_ln1_b, l1_mb_in, l1_mb_out, l1_ln2_g, l1_ln2_b, l1_ffn2_in, l1_ffn2_out, l1_ln3_g, l1_ln3_b):
    B, S, D = x.shape
    assert B == 1 and S % MB_BLOCK == 0 and S % FFN_TM == 0
    bf = lambda w: w.astype(BF16)
    h = x.reshape(S, D)

    h, (hg_in, hg_out, f2_in, f2_out) = _ffn_ln(
        h, bf(l0_ffn1_in), bf(l0_ffn1_out), l0_ln1_g, l0_ln1_b,
        cast_weights=(l0_hg_in, l0_hg_out, l0_ffn2_in, l0_ffn2_out))
    proj = _proj(h, hg_in, F32)
    mix = _hgrn_core(proj, lb_logits, l0_hg_norm_g)
    h = _out_ln(h, mix, hg_out, l0_ln2_g, l0_ln2_b)
    h, (f1_in, f1_out) = _ffn_ln(h, f2_in, f2_out, l0_ln3_g, l0_ln3_b,
                                 cast_weights=(l1_ffn1_in, l1_ffn1_out))

    h, (mb_in, mb_out, f2_in, f2_out) = _ffn_ln(
        h, f1_in, f1_out, l1_ln1_g, l1_ln1_b,
        cast_weights=(l1_mb_in, l1_mb_out, l1_ffn2_in, l1_ffn2_out))
    qkv = _proj(h, mb_in, BF16)
    mix = _moba_core(qkv)
    h = _out_ln(h, mix, mb_out, l1_ln2_g, l1_ln2_b)
    h, _ = _ffn_ln(h, f2_in, f2_out, l1_ln3_g, l1_ln3_b)
    return h.reshape(B, S, D)
```

```python
import functools

import jax
import jax.numpy as jnp
from jax import lax
from jax.experimental import pallas as pl
from jax.experimental.pallas import tpu as pltpu

F32 = jnp.float32
BF16 = jnp.bfloat16

DEPTH = 2
ALPHA = float((2 * DEPTH) ** 0.25)
LN_EPS = 1e-5
RMS_EPS = 1e-6
LOG2_E = 1.4426950408889634
HEAD_DIM = 128
SUBLANES = 8
BF16_SUBLANES = 16
LANES = 128
MB_BLOCK = 256
MB_TOPK = 3
MB_UNROLL = 4
MB_HEADS = 2
MB_MAX_FLOOR = -1e30

VMEM_LIMIT_BYTES = 56 * 1024 * 1024

FFN_TM = 512
FFN_TF = 256
PROJ_TM = 1024
PROJ_TN = 512
OUT_TM = 512
OUT_TK = 1024
HG_TB = 256
HG_CHUNK = 64
HG_HB = 8
HG_SAFE_LOG = 60.0


def _params(*sem):
    return pltpu.CompilerParams(dimension_semantics=sem, vmem_limit_bytes=VMEM_LIMIT_BYTES)


LN_ROWS = 128
ACC_TN = 1024


def _layer_norm_inplace(o_ref, g_ref, b_ref):
    g = g_ref[...]
    b = b_ref[...]

    def body(r, carry):
        rows = pl.ds(pl.multiple_of(r * LN_ROWS, LN_ROWS), LN_ROWS)
        y = o_ref[rows, :]
        mu = jnp.mean(y, axis=-1, keepdims=True)
        yc = y - mu
        var = jnp.mean(yc * yc, axis=-1, keepdims=True)
        o_ref[rows, :] = yc * lax.rsqrt(var + LN_EPS) * g + b
        return carry

    lax.fori_loop(0, o_ref.shape[0] // LN_ROWS, body, 0)


def _accumulate_dot(o_ref, a, w_ref):
    tn = min(ACC_TN, o_ref.shape[1])
    assert o_ref.shape[1] % tn == 0
    for c in range(o_ref.shape[1] // tn):
        cols = slice(c * tn, (c + 1) * tn)
        o_ref[:, cols] += jnp.dot(a, w_ref[:, cols], preferred_element_type=F32)


def _ffn_ln_kernel(n_cast, x_ref, wg_ref, wu_ref, wo_ref, g_ref, b_ref, *refs):
    cast_in = refs[:n_cast]
    o_ref = refs[n_cast]
    cast_out = refs[n_cast + 1: 2 * n_cast + 1]
    xb_ref = refs[2 * n_cast + 1]
    j = pl.program_id(1)

    @pl.when(j == 0)
    def _():
        x = x_ref[...]
        xb_ref[...] = x.astype(BF16)
        o_ref[...] = ALPHA * x

    xb = xb_ref[...]
    gate = jnp.dot(xb, wg_ref[...], preferred_element_type=F32)
    up = jnp.dot(xb, wu_ref[...], preferred_element_type=F32)
    h = (0.5 * gate * jax.nn.sigmoid(gate) * up).astype(BF16)
    _accumulate_dot(o_ref, h, wo_ref)

    for src, dst in zip(cast_in, cast_out):
        dst[...] = src[...].astype(BF16)

    @pl.when(j == pl.num_programs(1) - 1)
    def _():
        _layer_norm_inplace(o_ref, g_ref, b_ref)


def _cast_tile_spec(w, ni, nj):
    R, C = w.shape
    rows = R // ni
    assert rows * ni == R and rows % BF16_SUBLANES == 0
    width = next(c for c in (LANES, 2 * LANES, 4 * LANES, 8 * LANES) if C % c == 0 and C // c <= nj)
    ncols = C // width
    return pl.BlockSpec((rows, width), lambda i, j: (i, jnp.minimum(j, ncols - 1)))


def _ffn_ln(x, w_in, w_out, g, b, cast_weights=()):
    S, D = x.shape
    F = w_out.shape[0]
    nf = F // FFN_TF
    ni = S // FFN_TM
    cast_specs = [_cast_tile_spec(w, ni, nf) for w in cast_weights]
    outs = pl.pallas_call(
        functools.partial(_ffn_ln_kernel, len(cast_weights)),
        grid=(ni, nf),
        in_specs=[
            pl.BlockSpec((FFN_TM, D), lambda i, j: (i, 0)),
            pl.BlockSpec((D, FFN_TF), lambda i, j: (0, j)),
            pl.BlockSpec((D, FFN_TF), lambda i, j: (0, j + nf)),
            pl.BlockSpec((FFN_TF, D), lambda i, j: (j, 0)),
            pl.BlockSpec((1, D), lambda i, j: (0, 0)),
            pl.BlockSpec((1, D), lambda i, j: (0, 0)),
        ] + cast_specs,
        out_specs=[pl.BlockSpec((FFN_TM, D), lambda i, j: (i, 0))] + cast_specs,
        out_shape=[jax.ShapeDtypeStruct((S, D), F32)]
                  + [jax.ShapeDtypeStruct(w.shape, BF16) for w in cast_weights],
        scratch_shapes=[pltpu.VMEM((FFN_TM, D), BF16)],
        compiler_params=_params("parallel", "arbitrary"),
        name="ffn_ln",
    )(x, w_in, w_in, w_out, g.reshape(1, D), b.reshape(1, D), *cast_weights)
    return outs[0], tuple(outs[1:])


def _proj_kernel(x_ref, w_ref, o_ref, xb_ref):
    @pl.when(pl.program_id(1) == 0)
    def _():
        xb_ref[...] = x_ref[...].astype(BF16)

    o_ref[...] = jnp.dot(xb_ref[...], w_ref[...], preferred_element_type=F32).astype(o_ref.dtype)


def _proj(x, w, out_dtype):
    S, D = x.shape
    N = w.shape[1]
    return pl.pallas_call(
        _proj_kernel,
        grid=(S // PROJ_TM, N // PROJ_TN),
        in_specs=[
            pl.BlockSpec((PROJ_TM, D), lambda i, j: (i, 0)),
            pl.BlockSpec((D, PROJ_TN), lambda i, j: (0, j)),
        ],
        out_specs=pl.BlockSpec((PROJ_TM, PROJ_TN), lambda i, j: (i, j)),
        out_shape=jax.ShapeDtypeStruct((S, N), out_dtype),
        scratch_shapes=[pltpu.VMEM((PROJ_TM, D), BF16)],
        compiler_params=_params("parallel", "arbitrary"),
        name="proj",
    )(x, w)


def _out_ln_kernel(x_ref, a_ref, w_ref, g_ref, b_ref, o_ref):
    k = pl.program_id(1)

    @pl.when(k == 0)
    def _():
        o_ref[...] = ALPHA * x_ref[...]

    _accumulate_dot(o_ref, a_ref[...], w_ref)

    @pl.when(k == pl.num_programs(1) - 1)
    def _():
        _layer_norm_inplace(o_ref, g_ref, b_ref)


def _out_ln(x, a, w, g, b):
    S, D = x.shape
    K = a.shape[1]
    return pl.pallas_call(
        _out_ln_kernel,
        grid=(S // OUT_TM, K // OUT_TK),
        in_specs=[
            pl.BlockSpec((OUT_TM, D), lambda i, k: (i, 0)),
            pl.BlockSpec((OUT_TM, OUT_TK), lambda i, k: (i, k)),
            pl.BlockSpec((OUT_TK, D), lambda i, k: (k, 0)),
            pl.BlockSpec((1, D), lambda i, k: (0, 0)),
            pl.BlockSpec((1, D), lambda i, k: (0, 0)),
        ],
        out_specs=pl.BlockSpec((OUT_TM, D), lambda i, k: (i, 0)),
        out_shape=jax.ShapeDtypeStruct((S, D), F32),
        compiler_params=_params("parallel", "arbitrary"),
        name="out_ln",
    )(x, a, w, g.reshape(1, D), b.reshape(1, D))


def _nt_dot(a, b):
    return lax.dot_general(a, b, (((1,), (1,)), ((), ())), preferred_element_type=F32)


def _tn_dot(a, b):
    return lax.dot_general(a, b, (((0,), (0,)), ((), ())), preferred_element_type=F32)


def _hgrn_kernel(q_ref, f_ref, v_ref, g_ref, lb_ref, ng_ref, o_ref,
                 st_ref, cum_ref, qs_ref, oi_ref):
    TB, W = q_ref.shape
    C = HG_CHUNK
    nchunk = TB // C
    nhead = W // HEAD_DIM

    @pl.when(pl.program_id(1) == 0)
    def _():
        st_ref[...] = jnp.zeros_like(st_ref)

    lg = lb_ref[...]
    ex = jnp.exp(lg - jnp.max(lg, axis=0, keepdims=True))
    lb = ex[0:1, :] / jnp.sum(ex, axis=0, keepdims=True)
    qv = q_ref[...]
    q = qv * jax.nn.sigmoid(qv)
    forget = lb + (1.0 - lb) * jax.nn.sigmoid(f_ref[...])
    k = 1.0 - forget
    logf = jnp.log(forget)

    r = lax.broadcasted_iota(jnp.int32, (TB, TB), 0)
    c = lax.broadcasted_iota(jnp.int32, (TB, TB), 1)
    tri = jnp.where((c <= r) & (c // C == r // C), 1.0, 0.0).astype(BF16)
    hi = logf.astype(BF16)
    r1 = logf - hi.astype(F32)
    mid = r1.astype(BF16)
    lo = (r1 - mid.astype(F32)).astype(BF16)
    cum = (jnp.dot(tri, hi, preferred_element_type=F32)
           + jnp.dot(tri, mid, preferred_element_type=F32)
           + jnp.dot(tri, lo, preferred_element_type=F32))

    bound = jnp.zeros((1, W), F32)
    for ci in range(nchunk):
        mrow = cum[ci * C + C // 2 - 1: ci * C + C // 2, :]
        lrow = cum[ci * C + C - 1: ci * C + C, :]
        bound = jnp.maximum(bound, jnp.maximum(-mrow, mrow - lrow))
    safe = jnp.max(bound) <= HG_SAFE_LOG

    tr = lax.broadcasted_iota(jnp.int32, (C, C), 0)
    ts = lax.broadcasted_iota(jnp.int32, (C, C), 1)
    causal = ts <= tr

    gv = g_ref[...]
    og = ng_ref[...] * (gv * jax.nn.sigmoid(gv))

    def factorised_intra(rows, lanes):
        cu = cum[rows, lanes]
        m = cu[C // 2 - 1: C // 2, :]
        qd = (q[rows, lanes] * jnp.exp(cu - m)).astype(BF16)
        kd = (k[rows, lanes] * jnp.exp(m - cu)).astype(BF16)
        a = jnp.where(causal, _nt_dot(qd, kd), 0.0).astype(BF16)
        return jnp.dot(a, v_ref[rows, lanes].astype(BF16), preferred_element_type=F32)

    def stored_intra(rows, lanes):
        return oi_ref[rows, lanes]

    def recurrence(intra):
        for h in range(nhead):
            lanes = slice(h * HEAD_DIM, (h + 1) * HEAD_DIM)
            st = st_ref[h]
            for ci in range(nchunk):
                rows = slice(ci * C, (ci + 1) * C)
                cu = cum[rows, lanes]
                last = cu[C - 1: C, :]
                qe = (q[rows, lanes] * jnp.exp(cu)).astype(BF16)
                o = intra(rows, lanes) + _nt_dot(qe, st.astype(BF16))
                ke = (k[rows, lanes] * jnp.exp(last - cu)).astype(BF16)
                st = st * jnp.exp(last) + _tn_dot(v_ref[rows, lanes].astype(BF16), ke)
                o = o * lax.rsqrt(jnp.mean(o * o, axis=-1, keepdims=True) + RMS_EPS)
                o_ref[rows, lanes] = (o * og[rows, lanes]).astype(o_ref.dtype)
            st_ref[h] = st

    @pl.when(safe)
    def _():
        recurrence(factorised_intra)

    @pl.when(jnp.logical_not(safe))
    def _():
        cum_ref[...] = cum
        qs_ref[...] = q
        sidx = lax.broadcasted_iota(jnp.int32, (C, W), 0)
        for ci in range(nchunk):
            rows = slice(ci * C, (ci + 1) * C)
            cu = cum[rows, :]
            kc = k[rows, :]
            vc = v_ref[rows, :]

            def rows_body(t8, carry, ci=ci, cu=cu, kc=kc, vc=vc):
                base = pl.multiple_of(ci * C + t8 * SUBLANES, SUBLANES)
                c8 = cum_ref[pl.ds(base, SUBLANES), :]
                q8 = qs_ref[pl.ds(base, SUBLANES), :]
                out_rows = []
                for r in range(SUBLANES):
                    t = t8 * SUBLANES + r
                    dec = jnp.exp(jnp.where(sidx <= t, c8[r:r + 1, :] - cu, -jnp.inf))
                    prod = q8[r:r + 1, :] * dec * kc
                    heads = []
                    for h in range(nhead):
                        lanes = slice(h * HEAD_DIM, (h + 1) * HEAD_DIM)
                        sc = jnp.sum(prod[:, lanes], axis=1, keepdims=True)
                        heads.append(jnp.sum(sc * vc[:, lanes], axis=0, keepdims=True))
                    out_rows.append(jnp.concatenate(heads, axis=1))
                oi_ref[pl.ds(base, SUBLANES), :] = jnp.concatenate(out_rows, axis=0)
                return carry

            lax.fori_loop(0, C // SUBLANES, rows_body, 0)
        recurrence(stored_intra)


def _hgrn_core(proj, lb_logits, norm_g):
    S, D4 = proj.shape
    D = D4 // 4
    W = HG_HB * HEAD_DIM
    ng = D // W
    return pl.pallas_call(
        _hgrn_kernel,
        grid=(ng, S // HG_TB),
        in_specs=[
            pl.BlockSpec((HG_TB, W), lambda hg, t: (t, hg)),
            pl.BlockSpec((HG_TB, W), lambda hg, t: (t, hg + ng)),
            pl.BlockSpec((HG_TB, W), lambda hg, t: (t, hg + 2 * ng)),
            pl.BlockSpec((HG_TB, W), lambda hg, t: (t, hg + 3 * ng)),
            pl.BlockSpec((lb_logits.shape[0], W), lambda hg, t: (0, hg)),
            pl.BlockSpec((1, W), lambda hg, t: (0, hg)),
        ],
        out_specs=pl.BlockSpec((HG_TB, W), lambda hg, t: (t, hg)),
        out_shape=jax.ShapeDtypeStruct((S, D), BF16),
        scratch_shapes=[
            pltpu.VMEM((HG_HB, HEAD_DIM, HEAD_DIM), F32),
            pltpu.VMEM((HG_TB, W), F32),
            pltpu.VMEM((HG_TB, W), F32),
            pltpu.VMEM((HG_TB, W), F32),
        ],
        compiler_params=_params("parallel", "arbitrary"),
        name="hgrn_core",
    )(proj, proj, proj, proj, lb_logits.astype(F32), norm_g.reshape(1, D))


def _moba_kernel(q_ref, k_ref, v_ref, o_ref, km_ref, sel_ref, s_ref, vt_ref, acc_ref):
    L = MB_BLOCK
    S = k_ref.shape[0]
    nb = S // L
    c2 = HEAD_DIM ** -0.5 * LOG2_E
    last_group = nb // MB_UNROLL - 1

    km_ref[...] = jnp.zeros_like(km_ref)

    def mean_body(n, carry):
        rows = pl.ds(pl.multiple_of(n * L, L), L)
        for h in range(MB_HEADS):
            lanes = slice(h * HEAD_DIM, (h + 1) * HEAD_DIM)
            km_ref[h, pl.ds(n + 1, 1), :] = jnp.mean(k_ref[rows, lanes].astype(F32), axis=0, keepdims=True)
            vt_ref[h, n] = v_ref[rows, lanes].astype(F32).T.astype(BF16)
        return carry
    lax.fori_loop(0, nb, mean_body, 0)

    def tile_body(i, carry):
        _moba_tile(i, q_ref, k_ref, vt_ref, o_ref, km_ref, sel_ref, s_ref, acc_ref, c2, last_group)
        return carry
    lax.fori_loop(0, nb, tile_body, 0)


def _moba_tile(i, q_ref, k_ref, vt_ref, o_ref, km_ref, sel_ref, s_ref, acc_ref, c2, last_group):
    L = MB_BLOCK
    nrow = sel_ref.shape[1]
    heads = range(MB_HEADS)
    lanes = [slice(h * HEAD_DIM, (h + 1) * HEAD_DIM) for h in heads]
    own = pl.ds(pl.multiple_of(i * L, L), L)
    qt32 = [q_ref[own, lanes[h]].astype(F32).T for h in heads]
    qt_raw = [x.astype(BF16) for x in qt32]
    qt = [(x * c2).astype(BF16) for x in qt32]

    ridx = lax.broadcasted_iota(jnp.int32, (nrow, L), 0)
    for h in heads:
        gate = jnp.dot(km_ref[h].astype(BF16), qt_raw[h], preferred_element_type=F32)
        gate = jnp.where((ridx >= 1) & (ridx <= i), gate, -jnp.inf)
        sel = jnp.where(ridx == 0, 1.0, 0.0)
        for _ in range(MB_TOPK):
            mx = jnp.max(gate, axis=0, keepdims=True)
            first = jnp.min(jnp.where(gate == mx, ridx, nrow), axis=0, keepdims=True)
            pick = (ridx == first) & (mx > -jnp.inf)
            sel = jnp.where(pick, 1.0, sel)
            gate = jnp.where(pick, -jnp.inf, gate)
        sel_ref[h] = sel

    kpos = lax.broadcasted_iota(jnp.int32, (L, L), 0)
    qpos = lax.broadcasted_iota(jnp.int32, (L, L), 1)

    def key_block(r):
        return jnp.where(r == 0, i, r - 1)

    def score_member(h, r, slot, u, colmax, causal):
        n = key_block(r)
        s = jnp.dot(k_ref[pl.ds(pl.multiple_of(n * L, L), L), lanes[h]], qt[h],
                    preferred_element_type=F32)
        if causal:
            s = jnp.where(kpos <= qpos, s, -jnp.inf)
        s_ref[h, slot, u] = s
        on = sel_ref[h, pl.ds(r, 1), :] > 0.0
        return jnp.maximum(colmax, jnp.where(on, jnp.max(s, axis=0, keepdims=True), -jnp.inf))

    def score_group(g, slot, first_group=False):
        colmax = [jnp.full((1, L), -jnp.inf, F32) for _ in heads]
        for u in range(MB_UNROLL):
            r = g * MB_UNROLL + u
            for h in heads:
                colmax[h] = score_member(h, r, slot, u, colmax[h], causal=first_group and u == 0)
        return colmax

    def attend_group(g, slot, colmax, m, l):
        m, l = list(m), list(l)
        for h in heads:
            m_new = jnp.maximum(m[h], colmax[h])
            alpha = jnp.exp2(m[h] - m_new)
            lh = alpha * l[h]
            ah = alpha * acc_ref[h]
            for u in range(MB_UNROLL):
                r = g * MB_UNROLL + u
                on = sel_ref[h, pl.ds(r, 1), :] > 0.0
                p = jnp.exp2(s_ref[h, slot, u] - jnp.where(on, m_new, jnp.inf))
                lh = lh + jnp.sum(p, axis=0, keepdims=True)
                ah = ah + jnp.dot(vt_ref[h, key_block(r)], p.astype(BF16), preferred_element_type=F32)
            m[h], l[h] = m_new, lh
            acc_ref[h] = ah
        return m, l

    def pair_body(pi, carry):
        m, l, cm_even = carry
        g = 2 * pi
        cm_odd = score_group(g + 1, 1)
        m, l = attend_group(g, 0, cm_even, m, l)
        cm_even = score_group(jnp.minimum(g + 2, last_group), 0)
        m, l = attend_group(g + 1, 1, cm_odd, m, l)
        return m, l, cm_even

    m0 = [jnp.full((1, L), MB_MAX_FLOOR, F32) for _ in heads]
    l0 = [jnp.zeros((1, L), F32) for _ in heads]
    acc_ref[...] = jnp.zeros_like(acc_ref)
    ngroups = i // MB_UNROLL + 1
    m, l, cm_even = lax.fori_loop(0, ngroups // 2, pair_body,
                                  (m0, l0, score_group(0, 0, first_group=True)))
    m, l = lax.cond(ngroups % 2 == 1,
                    lambda: attend_group(ngroups - 1, 0, cm_even, m, l),
                    lambda: (list(m), list(l)))
    for h in heads:
        o_ref[own, lanes[h]] = (acc_ref[h] / l[h]).T.astype(o_ref.dtype)


def _moba_core(qkv):
    S, D3 = qkv.shape
    D = D3 // 3
    L = MB_BLOCK
    nb = S // L
    W = MB_HEADS * HEAD_DIM
    ng = D // W
    nrow = nb + SUBLANES
    assert nb % (2 * MB_UNROLL) == 0 and ng * W == D
    return pl.pallas_call(
        _moba_kernel,
        grid=(ng,),
        in_specs=[
            pl.BlockSpec((S, W), lambda hg: (0, hg)),
            pl.BlockSpec((S, W), lambda hg: (0, ng + hg)),
            pl.BlockSpec((S, W), lambda hg: (0, 2 * ng + hg)),
        ],
        out_specs=pl.BlockSpec((S, W), lambda hg: (0, hg)),
        out_shape=jax.ShapeDtypeStruct((S, D), BF16),
        scratch_shapes=[
            pltpu.VMEM((MB_HEADS, nrow, HEAD_DIM), F32),
            pltpu.VMEM((MB_HEADS, nrow, L), F32),
            pltpu.VMEM((MB_HEADS, 2, MB_UNROLL, L, L), F32),
            pltpu.VMEM((MB_HEADS, nb, HEAD_DIM, L), BF16),
            pltpu.VMEM((MB_HEADS, HEAD_DIM, L), F32),
        ],
        compiler_params=_params("parallel"),
        name="moba_core",
    )(qkv, qkv, qkv)


def kernel(x, lb_logits, l0_ffn1_in, l0_ffn1_out, l0_ln1_g, l0_ln1_b, l0_hg_in, l0_hg_norm_g, l0_hg_out, l0_ln2_g, l0_ln2_b, l0_ffn2_in, l0_ffn2_out, l0_ln3_g, l0_ln3_b, l1_ffn1_in, l1_ffn1_out, l1_ln1_g, l1_ln1_b, l1_mb_in, l1_mb_out, l1_ln2_g, l1_ln2_b, l1_ffn2_in, l1_ffn2_out, l1_ln3_g, l1_ln3_b):
    B, S, D = x.shape
    assert B == 1 and S % MB_BLOCK == 0 and S % FFN_TM == 0
    bf = lambda w: w.astype(BF16)
    h = x.reshape(S, D)

    h, (hg_in, hg_out, f2_in, f2_out) = _ffn_ln(
        h, bf(l0_ffn1_in), bf(l0_ffn1_out), l0_ln1_g, l0_ln1_b,
        cast_weights=(l0_hg_in, l0_hg_out, l0_ffn2_in, l0_ffn2_out))
    proj = _proj(h, hg_in, F32)
    mix = _hgrn_core(proj, lb_logits, l0_hg_norm_g)
    h = _out_ln(h, mix, hg_out, l0_ln2_g, l0_ln2_b)
    h, (f1_in, f1_out) = _ffn_ln(h, f2_in, f2_out, l0_ln3_g, l0_ln3_b,
                                 cast_weights=(l1_ffn1_in, l1_ffn1_out))

    h, (mb_in, mb_out, f2_in, f2_out) = _ffn_ln(
        h, f1_in, f1_out, l1_ln1_g, l1_ln1_b,
        cast_weights=(l1_mb_in, l1_mb_out, l1_ffn2_in, l1_ffn2_out))
    qkv = _proj(h, mb_in, BF16)
    mix = _moba_core(qkv)
    h = _out_ln(h, mix, mb_out, l1_ln2_g, l1_ln2_b)
    h, _ = _ffn_ln(h, f2_in, f2_out, l1_ln3_g, l1_ln3_b)
    return h.reshape(B, S, D)
```

```python
import functools

import jax
import jax.numpy as jnp
from jax import lax
from jax.experimental import pallas as pl
from jax.experimental.pallas import tpu as pltpu

F32 = jnp.float32
BF16 = jnp.bfloat16

DEPTH = 2
ALPHA = float((2 * DEPTH) ** 0.25)
LN_EPS = 1e-5
RMS_EPS = 1e-6
LOG2_E = 1.4426950408889634
HEAD_DIM = 128
SUBLANES = 8
BF16_SUBLANES = 16
LANES = 128
MB_BLOCK = 256
MB_TOPK = 3
MB_UNROLL = 4
MB_HEADS = 2
MB_MAX_FLOOR = -1e30

VMEM_LIMIT_BYTES = 56 * 1024 * 1024

FFN_TM = 512
FFN_TF = 256
PROJ_TM = 1024
PROJ_TN = 512
OUT_TM = 512
OUT_TK = 1024
HG_TB = 256
HG_CHUNK = 64
HG_HB = 8
HG_SAFE_LOG = 60.0


def _params(*sem):
    return pltpu.CompilerParams(dimension_semantics=sem, vmem_limit_bytes=VMEM_LIMIT_BYTES)


LN_ROWS = 128
ACC_TN = 1024


def _layer_norm_inplace(o_ref, g_ref, b_ref):
    g = g_ref[...]
    b = b_ref[...]

    def body(r, carry):
        rows = pl.ds(pl.multiple_of(r * LN_ROWS, LN_ROWS), LN_ROWS)
        y = o_ref[rows, :]
        mu = jnp.mean(y, axis=-1, keepdims=True)
        yc = y - mu
        var = jnp.mean(yc * yc, axis=-1, keepdims=True)
        o_ref[rows, :] = yc * lax.rsqrt(var + LN_EPS) * g + b
        return carry

    lax.fori_loop(0, o_ref.shape[0] // LN_ROWS, body, 0)


def _accumulate_dot(o_ref, a, w_ref):
    tn = min(ACC_TN, o_ref.shape[1])
    assert o_ref.shape[1] % tn == 0
    for c in range(o_ref.shape[1] // tn):
        cols = slice(c * tn, (c + 1) * tn)
        o_ref[:, cols] += jnp.dot(a, w_ref[:, cols], preferred_element_type=F32)


def _ffn_ln_kernel(n_cast, x_ref, wg_ref, wu_ref, wo_ref, g_ref, b_ref, *refs):
    cast_in = refs[:n_cast]
    o_ref = refs[n_cast]
    cast_out = refs[n_cast + 1: 2 * n_cast + 1]
    xb_ref = refs[2 * n_cast + 1]
    j = pl.program_id(1)

    @pl.when(j == 0)
    def _():
        x = x_ref[...]
        xb_ref[...] = x.astype(BF16)
        o_ref[...] = ALPHA * x

    xb = xb_ref[...]
    gate = jnp.dot(xb, wg_ref[...], preferred_element_type=F32)
    up = jnp.dot(xb, wu_ref[...], preferred_element_type=F32)
    h = (0.5 * gate * jax.nn.sigmoid(gate) * up).astype(BF16)
    _accumulate_dot(o_ref, h, wo_ref)

    for src, dst in zip(cast_in, cast_out):
        dst[...] = src[...].astype(BF16)

    @pl.when(j == pl.num_programs(1) - 1)
    def _():
        _layer_norm_inplace(o_ref, g_ref, b_ref)


def _cast_tile_spec(w, ni, nj):
    R, C = w.shape
    rows = R // ni
    assert rows * ni == R and rows % BF16_SUBLANES == 0
    width = next(c for c in (LANES, 2 * LANES, 4 * LANES, 8 * LANES) if C % c == 0 and C // c <= nj)
    ncols = C // width
    return pl.BlockSpec((rows, width), lambda i, j: (i, jnp.minimum(j, ncols - 1)))


def _ffn_ln(x, w_in, w_out, g, b, cast_weights=()):
    S, D = x.shape
    F = w_out.shape[0]
    nf = F // FFN_TF
    ni = S // FFN_TM
    cast_specs = [_cast_tile_spec(w, ni, nf) for w in cast_weights]
    outs = pl.pallas_call(
        functools.partial(_ffn_ln_kernel, len(cast_weights)),
        grid=(ni, nf),
        in_specs=[
            pl.BlockSpec((FFN_TM, D), lambda i, j: (i, 0)),
            pl.BlockSpec((D, FFN_TF), lambda i, j: (0, j)),
            pl.BlockSpec((D, FFN_TF), lambda i, j: (0, j + nf)),
            pl.BlockSpec((FFN_TF, D), lambda i, j: (j, 0)),
            pl.BlockSpec((1, D), lambda i, j: (0, 0)),
            pl.BlockSpec((1, D), lambda i, j: (0, 0)),
        ] + cast_specs,
        out_specs=[pl.BlockSpec((FFN_TM, D), lambda i, j: (i, 0))] + cast_specs,
        out_shape=[jax.ShapeDtypeStruct((S, D), F32)]
                  + [jax.ShapeDtypeStruct(w.shape, BF16) for w in cast_weights],
        scratch_shapes=[pltpu.VMEM((FFN_TM, D), BF16)],
        compiler_params=_params("parallel", "arbitrary"),
        name="ffn_ln",
    )(x, w_in, w_in, w_out, g.reshape(1, D), b.reshape(1, D), *cast_weights)
    return outs[0], tuple(outs[1:])


def _proj_kernel(x_ref, w_ref, o_ref, xb_ref):
    @pl.when(pl.program_id(1) == 0)
    def _():
        xb_ref[...] = x_ref[...].astype(BF16)

    o_ref[...] = jnp.dot(xb_ref[...], w_ref[...], preferred_element_type=F32).astype(o_ref.dtype)


def _proj(x, w, out_dtype):
    S, D = x.shape
    N = w.shape[1]
    return pl.pallas_call(
        _proj_kernel,
        grid=(S // PROJ_TM, N // PROJ_TN),
        in_specs=[
            pl.BlockSpec((PROJ_TM, D), lambda i, j: (i, 0)),
            pl.BlockSpec((D, PROJ_TN), lambda i, j: (0, j)),
        ],
        out_specs=pl.BlockSpec((PROJ_TM, PROJ_TN), lambda i, j: (i, j)),
        out_shape=jax.ShapeDtypeStruct((S, N), out_dtype),
        scratch_shapes=[pltpu.VMEM((PROJ_TM, D), BF16)],
        compiler_params=_params("parallel", "arbitrary"),
        name="proj",
    )(x, w)


def _out_ln_kernel(x_ref, a_ref, w_ref, g_ref, b_ref, o_ref):
    k = pl.program_id(1)

    @pl.when(k == 0)
    def _():
        o_ref[...] = ALPHA * x_ref[...]

    _accumulate_dot(o_ref, a_ref[...], w_ref)

    @pl.when(k == pl.num_programs(1) - 1)
    def _():
        _layer_norm_inplace(o_ref, g_ref, b_ref)


def _out_ln(x, a, w, g, b):
    S, D = x.shape
    K = a.shape[1]
    return pl.pallas_call(
        _out_ln_kernel,
        grid=(S // OUT_TM, K // OUT_TK),
        in_specs=[
            pl.BlockSpec((OUT_TM, D), lambda i, k: (i, 0)),
            pl.BlockSpec((OUT_TM, OUT_TK), lambda i, k: (i, k)),
            pl.BlockSpec((OUT_TK, D), lambda i, k: (k, 0)),
            pl.BlockSpec((1, D), lambda i, k: (0, 0)),
            pl.BlockSpec((1, D), lambda i, k: (0, 0)),
        ],
        out_specs=pl.BlockSpec((OUT_TM, D), lambda i, k: (i, 0)),
        out_shape=jax.ShapeDtypeStruct((S, D), F32),
        compiler_params=_params("parallel", "arbitrary"),
        name="out_ln",
    )(x, a, w, g.reshape(1, D), b.reshape(1, D))


def _nt_dot(a, b):
    return lax.dot_general(a, b, (((1,), (1,)), ((), ())), preferred_element_type=F32)


def _tn_dot(a, b):
    return lax.dot_general(a, b, (((0,), (0,)), ((), ())), preferred_element_type=F32)


def _hgrn_kernel(q_ref, f_ref, v_ref, g_ref, lb_ref, ng_ref, o_ref,
                 st_ref, cum_ref, qs_ref, ks_ref, og_ref, oi_ref):
    TB, W = q_ref.shape
    C = HG_CHUNK
    nchunk = TB // C
    nhead = W // HEAD_DIM

    @pl.when(pl.program_id(1) == 0)
    def _():
        st_ref[...] = jnp.zeros_like(st_ref)

    lg = lb_ref[...]
    ex = jnp.exp(lg - jnp.max(lg, axis=0, keepdims=True))
    lb = ex[0:1, :] / jnp.sum(ex, axis=0, keepdims=True)
    qv = q_ref[...]
    q = qv * jax.nn.sigmoid(qv)
    forget = lb + (1.0 - lb) * jax.nn.sigmoid(f_ref[...])
    k = 1.0 - forget
    logf = jnp.log(forget)

    r = lax.broadcasted_iota(jnp.int32, (TB, TB), 0)
    c = lax.broadcasted_iota(jnp.int32, (TB, TB), 1)
    tri = jnp.where((c <= r) & (c // C == r // C), 1.0, 0.0).astype(BF16)
    hi = logf.astype(BF16)
    r1 = logf - hi.astype(F32)
    mid = r1.astype(BF16)
    lo = (r1 - mid.astype(F32)).astype(BF16)
    cum = (jnp.dot(tri, hi, preferred_element_type=F32)
           + jnp.dot(tri, mid, preferred_element_type=F32)
           + jnp.dot(tri, lo, preferred_element_type=F32))

    cum_ref[...] = cum
    qs_ref[...] = q
    ks_ref[...] = k

    bound = jnp.zeros((1, W), F32)
    for ci in range(nchunk):
        mrow = cum[ci * C + C // 2 - 1: ci * C + C // 2, :]
        lrow = cum[ci * C + C - 1: ci * C + C, :]
        bound = jnp.maximum(bound, jnp.maximum(-mrow, mrow - lrow))
    safe = jnp.max(bound) <= HG_SAFE_LOG

    tr = lax.broadcasted_iota(jnp.int32, (C, C), 0)
    ts = lax.broadcasted_iota(jnp.int32, (C, C), 1)
    causal = ts <= tr

    gv = g_ref[...]
    og_ref[...] = ng_ref[...] * (gv * jax.nn.sigmoid(gv))

    def factorised_intra(rows, lanes):
        cu = cum_ref[rows, lanes]
        m = cu[C // 2 - 1: C // 2, :]
        qd = (qs_ref[rows, lanes] * jnp.exp(cu - m)).astype(BF16)
        kd = (ks_ref[rows, lanes] * jnp.exp(m - cu)).astype(BF16)
        a = jnp.where(causal, _nt_dot(qd, kd), 0.0).astype(BF16)
        return jnp.dot(a, v_ref[rows, lanes].astype(BF16), preferred_element_type=F32)

    def stored_intra(rows, lanes):
        return oi_ref[rows, lanes]

    def recurrence(intra):
        for h in range(nhead):
            lanes = slice(h * HEAD_DIM, (h + 1) * HEAD_DIM)
            st = st_ref[h]
            for ci in range(nchunk):
                rows = slice(ci * C, (ci + 1) * C)
                cu = cum_ref[rows, lanes]
                last = cu[C - 1: C, :]
                qe = (qs_ref[rows, lanes] * jnp.exp(cu)).astype(BF16)
                o = intra(rows, lanes) + _nt_dot(qe, st.astype(BF16))
                ke = (ks_ref[rows, lanes] * jnp.exp(last - cu)).astype(BF16)
                st = st * jnp.exp(last) + _tn_dot(v_ref[rows, lanes].astype(BF16), ke)
                o = o * lax.rsqrt(jnp.mean(o * o, axis=-1, keepdims=True) + RMS_EPS)
                o_ref[rows, lanes] = (o * og_ref[rows, lanes]).astype(o_ref.dtype)
            st_ref[h] = st

    @pl.when(safe)
    def _():
        recurrence(factorised_intra)

    @pl.when(jnp.logical_not(safe))
    def _():
        sidx = lax.broadcasted_iota(jnp.int32, (C, W), 0)
        for ci in range(nchunk):
            rows = slice(ci * C, (ci + 1) * C)
            cu = cum_ref[rows, :]
            kc = ks_ref[rows, :]
            vc = v_ref[rows, :]

            def rows_body(t8, carry, ci=ci, cu=cu, kc=kc, vc=vc):
                base = pl.multiple_of(ci * C + t8 * SUBLANES, SUBLANES)
                c8 = cum_ref[pl.ds(base, SUBLANES), :]
                q8 = qs_ref[pl.ds(base, SUBLANES), :]
                out_rows = []
                for r in range(SUBLANES):
                    t = t8 * SUBLANES + r
                    dec = jnp.exp(jnp.where(sidx <= t, c8[r:r + 1, :] - cu, -jnp.inf))
                    prod = q8[r:r + 1, :] * dec * kc
                    heads = []
                    for h in range(nhead):
                        lanes = slice(h * HEAD_DIM, (h + 1) * HEAD_DIM)
                        sc = jnp.sum(prod[:, lanes], axis=1, keepdims=True)
                        heads.append(jnp.sum(sc * vc[:, lanes], axis=0, keepdims=True))
                    out_rows.append(jnp.concatenate(heads, axis=1))
                oi_ref[pl.ds(base, SUBLANES), :] = jnp.concatenate(out_rows, axis=0)
                return carry

            lax.fori_loop(0, C // SUBLANES, rows_body, 0)
        recurrence(stored_intra)


def _hgrn_core(proj, lb_logits, norm_g):
    S, D4 = proj.shape
    D = D4 // 4
    W = HG_HB * HEAD_DIM
    ng = D // W
    return pl.pallas_call(
        _hgrn_kernel,
        grid=(ng, S // HG_TB),
        in_specs=[
            pl.BlockSpec((HG_TB, W), lambda hg, t: (t, hg)),
            pl.BlockSpec((HG_TB, W), lambda hg, t: (t, hg + ng)),
            pl.BlockSpec((HG_TB, W), lambda hg, t: (t, hg + 2 * ng)),
            pl.BlockSpec((HG_TB, W), lambda hg, t: (t, hg + 3 * ng)),
            pl.BlockSpec((lb_logits.shape[0], W), lambda hg, t: (0, hg)),
            pl.BlockSpec((1, W), lambda hg, t: (0, hg)),
        ],
        out_specs=pl.BlockSpec((HG_TB, W), lambda hg, t: (t, hg)),
        out_shape=jax.ShapeDtypeStruct((S, D), BF16),
        scratch_shapes=[
            pltpu.VMEM((HG_HB, HEAD_DIM, HEAD_DIM), F32),
        ] + [pltpu.VMEM((HG_TB, W), F32)] * 5,
        compiler_params=_params("parallel", "arbitrary"),
        name="hgrn_core",
    )(proj, proj, proj, proj, lb_logits.astype(F32), norm_g.reshape(1, D))


def _moba_kernel(q_ref, k_ref, v_ref, o_ref, km_ref, sel_ref, s_ref, vt_ref, acc_ref):
    L = MB_BLOCK
    S = k_ref.shape[0]
    nb = S // L
    c2 = HEAD_DIM ** -0.5 * LOG2_E
    last_group = nb // MB_UNROLL - 1

    km_ref[...] = jnp.zeros_like(km_ref)

    def mean_body(n, carry):
        rows = pl.ds(pl.multiple_of(n * L, L), L)
        for h in range(MB_HEADS):
            lanes = slice(h * HEAD_DIM, (h + 1) * HEAD_DIM)
            km_ref[h, pl.ds(n + 1, 1), :] = jnp.mean(k_ref[rows, lanes].astype(F32), axis=0, keepdims=True)
            vt_ref[h, n] = v_ref[rows, lanes].astype(F32).T.astype(BF16)
        return carry
    lax.fori_loop(0, nb, mean_body, 0)

    def tile_body(i, carry):
        _moba_tile(i, q_ref, k_ref, vt_ref, o_ref, km_ref, sel_ref, s_ref, acc_ref, c2, last_group)
        return carry
    lax.fori_loop(0, nb, tile_body, 0)


def _moba_tile(i, q_ref, k_ref, vt_ref, o_ref, km_ref, sel_ref, s_ref, acc_ref, c2, last_group):
    L = MB_BLOCK
    nrow = sel_ref.shape[1]
    heads = range(MB_HEADS)
    lanes = [slice(h * HEAD_DIM, (h + 1) * HEAD_DIM) for h in heads]
    own = pl.ds(pl.multiple_of(i * L, L), L)
    qt32 = [q_ref[own, lanes[h]].astype(F32).T for h in heads]
    qt_raw = [x.astype(BF16) for x in qt32]
    qt = [(x * c2).astype(BF16) for x in qt32]

    ridx = lax.broadcasted_iota(jnp.int32, (nrow, L), 0)
    for h in heads:
        gate = jnp.dot(km_ref[h].astype(BF16), qt_raw[h], preferred_element_type=F32)
        gate = jnp.where((ridx >= 1) & (ridx <= i), gate, -jnp.inf)
        sel = jnp.where(ridx == 0, 1.0, 0.0)
        for _ in range(MB_TOPK):
            mx = jnp.max(gate, axis=0, keepdims=True)
            first = jnp.min(jnp.where(gate == mx, ridx, nrow), axis=0, keepdims=True)
            pick = (ridx == first) & (mx > -jnp.inf)
            sel = jnp.where(pick, 1.0, sel)
            gate = jnp.where(pick, -jnp.inf, gate)
        sel_ref[h] = sel

    kpos = lax.broadcasted_iota(jnp.int32, (L, L), 0)
    qpos = lax.broadcasted_iota(jnp.int32, (L, L), 1)

    def key_block(r):
        return jnp.where(r == 0, i, r - 1)

    def score_member(h, r, slot, u, colmax, causal):
        n = key_block(r)
        s = jnp.dot(k_ref[pl.ds(pl.multiple_of(n * L, L), L), lanes[h]], qt[h],
                    preferred_element_type=F32)
        if causal:
            s = jnp.where(kpos <= qpos, s, -jnp.inf)
        s_ref[h, slot, u] = s
        on = sel_ref[h, pl.ds(r, 1), :] > 0.0
        return jnp.maximum(colmax, jnp.where(on, jnp.max(s, axis=0, keepdims=True), -jnp.inf))

    def score_group(g, slot, first_group=False):
        colmax = [jnp.full((1, L), -jnp.inf, F32) for _ in heads]
        for u in range(MB_UNROLL):
            r = g * MB_UNROLL + u
            for h in heads:
                colmax[h] = score_member(h, r, slot, u, colmax[h], causal=first_group and u == 0)
        return colmax

    def attend_group(g, slot, colmax, m, l):
        m, l = list(m), list(l)
        for h in heads:
            m_new = jnp.maximum(m[h], colmax[h])
            alpha = jnp.exp2(m[h] - m_new)
            lh = alpha * l[h]
            ah = alpha * acc_ref[h]
            for u in range(MB_UNROLL):
                r = g * MB_UNROLL + u
                on = sel_ref[h, pl.ds(r, 1), :] > 0.0
                p = jnp.exp2(s_ref[h, slot, u] - jnp.where(on, m_new, jnp.inf))
                lh = lh + jnp.sum(p, axis=0, keepdims=True)
                ah = ah + jnp.dot(vt_ref[h, key_block(r)], p.astype(BF16), preferred_element_type=F32)
            m[h], l[h] = m_new, lh
            acc_ref[h] = ah
        return m, l

    def pair_body(pi, carry):
        m, l, cm_even = carry
        g = 2 * pi
        cm_odd = score_group(g + 1, 1)
        m, l = attend_group(g, 0, cm_even, m, l)
        cm_even = score_group(jnp.minimum(g + 2, last_group), 0)
        m, l = attend_group(g + 1, 1, cm_odd, m, l)
        return m, l, cm_even

    m0 = [jnp.full((1, L), MB_MAX_FLOOR, F32) for _ in heads]
    l0 = [jnp.zeros((1, L), F32) for _ in heads]
    acc_ref[...] = jnp.zeros_like(acc_ref)
    ngroups = i // MB_UNROLL + 1
    m, l, cm_even = lax.fori_loop(0, ngroups // 2, pair_body,
                                  (m0, l0, score_group(0, 0, first_group=True)))
    m, l = lax.cond(ngroups % 2 == 1,
                    lambda: attend_group(ngroups - 1, 0, cm_even, m, l),
                    lambda: (list(m), list(l)))
    for h in heads:
        o_ref[own, lanes[h]] = (acc_ref[h] / l[h]).T.astype(o_ref.dtype)


def _moba_core(qkv):
    S, D3 = qkv.shape
    D = D3 // 3
    L = MB_BLOCK
    nb = S // L
    W = MB_HEADS * HEAD_DIM
    ng = D // W
    nrow = nb + SUBLANES
    assert nb % (2 * MB_UNROLL) == 0 and ng * W == D
    return pl.pallas_call(
        _moba_kernel,
        grid=(ng,),
        in_specs=[
            pl.BlockSpec((S, W), lambda hg: (0, hg)),
            pl.BlockSpec((S, W), lambda hg: (0, ng + hg)),
            pl.BlockSpec((S, W), lambda hg: (0, 2 * ng + hg)),
        ],
        out_specs=pl.BlockSpec((S, W), lambda hg: (0, hg)),
        out_shape=jax.ShapeDtypeStruct((S, D), BF16),
        scratch_shapes=[
            pltpu.VMEM((MB_HEADS, nrow, HEAD_DIM), F32),
            pltpu.VMEM((MB_HEADS, nrow, L), F32),
            pltpu.VMEM((MB_HEADS, 2, MB_UNROLL, L, L), F32),
            pltpu.VMEM((MB_HEADS, nb, HEAD_DIM, L), BF16),
            pltpu.VMEM((MB_HEADS, HEAD_DIM, L), F32),
        ],
        compiler_params=_params("parallel"),
        name="moba_core",
    )(qkv, qkv, qkv)


def kernel(x, lb_logits, l0_ffn1_in, l0_ffn1_out, l0_ln1_g, l0_ln1_b, l0_hg_in, l0_hg_norm_g, l0_hg_out, l0_ln2_g, l0_ln2_b, l0_ffn2_in, l0_ffn2_out, l0_ln3_g, l0_ln3_b, l1_ffn1_in, l1_ffn1_out, l1_ln1_g, l1_ln1_b, l1_mb_in, l1_mb_out, l1_ln2_g, l1_ln2_b, l1_ffn2_in, l1_ffn2_out, l1_ln3_g, l1_ln3_b):
    B, S, D = x.shape
    assert B == 1 and S % MB_BLOCK == 0 and S % FFN_TM == 0
    bf = lambda w: w.astype(BF16)
    h = x.reshape(S, D)

    h, (hg_in, hg_out, f2_in, f2_out) = _ffn_ln(
        h, bf(l0_ffn1_in), bf(l0_ffn1_out), l0_ln1_g, l0_ln1_b,
        cast_weights=(l0_hg_in, l0_hg_out, l0_ffn2_in, l0_ffn2_out))
    proj = _proj(h, hg_in, F32)
    mix = _hgrn_core(proj, lb_logits, l0_hg_norm_g)
    h = _out_ln(h, mix, hg_out, l0_ln2_g, l0_ln2_b)
    h, (f1_in, f1_out) = _ffn_ln(h, f2_in, f2_out, l0_ln3_g, l0_ln3_b,
                                 cast_weights=(l1_ffn1_in, l1_ffn1_out))

    h, (mb_in, mb_out, f2_in, f2_out) = _ffn_ln(
        h, f1_in, f1_out, l1_ln1_g, l1_ln1_b,
        cast_weights=(l1_mb_in, l1_mb_out, l1_ffn2_in, l1_ffn2_out))
    qkv = _proj(h, mb_in, BF16)
    mix = _moba_core(qkv)
    h = _out_ln(h, mix, mb_out, l1_ln2_g, l1_ln2_b)
    h, _ = _ffn_ln(h, f2_in, f2_out, l1_ln3_g, l1_ln3_b)
    return h.reshape(B, S, D)
```

```python
import functools

import jax
import jax.numpy as jnp
from jax import lax
from jax.experimental import pallas as pl
from jax.experimental.pallas import tpu as pltpu

F32 = jnp.float32
BF16 = jnp.bfloat16

DEPTH = 2
ALPHA = float((2 * DEPTH) ** 0.25)
LN_EPS = 1e-5
RMS_EPS = 1e-6
LOG2_E = 1.4426950408889634
HEAD_DIM = 128
SUBLANES = 8
BF16_SUBLANES = 16
LANES = 128
MB_BLOCK = 256
MB_TOPK = 3
MB_UNROLL = 4
MB_HEADS = 2
MB_VROWS = HEAD_DIM + 16
MB_MAX_FLOOR = -1e30

VMEM_LIMIT_BYTES = 56 * 1024 * 1024

FFN_TM = 512
FFN_TF = 256
PROJ_TM = 1024
PROJ_TN = 512
OUT_TM = 512
OUT_TK = 1024
HG_TB = 256
HG_CHUNK = 64
HG_HB = 8
HG_SAFE_LOG = 60.0


def _params(*sem):
    return pltpu.CompilerParams(dimension_semantics=sem, vmem_limit_bytes=VMEM_LIMIT_BYTES)


LN_ROWS = 128
ACC_TN = 1024


def _layer_norm_inplace(o_ref, g_ref, b_ref):
    g = g_ref[...]
    b = b_ref[...]

    def body(r, carry):
        rows = pl.ds(pl.multiple_of(r * LN_ROWS, LN_ROWS), LN_ROWS)
        y = o_ref[rows, :]
        mu = jnp.mean(y, axis=-1, keepdims=True)
        yc = y - mu
        var = jnp.mean(yc * yc, axis=-1, keepdims=True)
        o_ref[rows, :] = yc * lax.rsqrt(var + LN_EPS) * g + b
        return carry

    lax.fori_loop(0, o_ref.shape[0] // LN_ROWS, body, 0)


def _accumulate_dot(o_ref, a, w_ref):
    tn = min(ACC_TN, o_ref.shape[1])
    assert o_ref.shape[1] % tn == 0
    for c in range(o_ref.shape[1] // tn):
        cols = slice(c * tn, (c + 1) * tn)
        o_ref[:, cols] += jnp.dot(a, w_ref[:, cols], preferred_element_type=F32)


def _ffn_ln_kernel(n_cast, x_ref, wg_ref, wu_ref, wo_ref, g_ref, b_ref, *refs):
    cast_in = refs[:n_cast]
    o_ref = refs[n_cast]
    cast_out = refs[n_cast + 1: 2 * n_cast + 1]
    xb_ref = refs[2 * n_cast + 1]
    j = pl.program_id(1)

    @pl.when(j == 0)
    def _():
        x = x_ref[...]
        xb_ref[...] = x.astype(BF16)
        o_ref[...] = ALPHA * x

    xb = xb_ref[...]
    gate = jnp.dot(xb, wg_ref[...], preferred_element_type=F32)
    up = jnp.dot(xb, wu_ref[...], preferred_element_type=F32)
    h = (0.5 * gate * jax.nn.sigmoid(gate) * up).astype(BF16)
    _accumulate_dot(o_ref, h, wo_ref)

    for src, dst in zip(cast_in, cast_out):
        dst[...] = src[...].astype(BF16)

    @pl.when(j == pl.num_programs(1) - 1)
    def _():
        _layer_norm_inplace(o_ref, g_ref, b_ref)


def _cast_tile_spec(w, ni, nj):
    R, C = w.shape
    rows = R // ni
    assert rows * ni == R and rows % BF16_SUBLANES == 0
    width = next(c for c in (LANES, 2 * LANES, 4 * LANES, 8 * LANES) if C % c == 0 and C // c <= nj)
    ncols = C // width
    return pl.BlockSpec((rows, width), lambda i, j: (i, jnp.minimum(j, ncols - 1)))


def _ffn_ln(x, w_in, w_out, g, b, cast_weights=()):
    S, D = x.shape
    F = w_out.shape[0]
    nf = F // FFN_TF
    ni = S // FFN_TM
    cast_specs = [_cast_tile_spec(w, ni, nf) for w in cast_weights]
    outs = pl.pallas_call(
        functools.partial(_ffn_ln_kernel, len(cast_weights)),
        grid=(ni, nf),
        in_specs=[
            pl.BlockSpec((FFN_TM, D), lambda i, j: (i, 0)),
            pl.BlockSpec((D, FFN_TF), lambda i, j: (0, j)),
            pl.BlockSpec((D, FFN_TF), lambda i, j: (0, j + nf)),
            pl.BlockSpec((FFN_TF, D), lambda i, j: (j, 0)),
            pl.BlockSpec((1, D), lambda i, j: (0, 0)),
            pl.BlockSpec((1, D), lambda i, j: (0, 0)),
        ] + cast_specs,
        out_specs=[pl.BlockSpec((FFN_TM, D), lambda i, j: (i, 0))] + cast_specs,
        out_shape=[jax.ShapeDtypeStruct((S, D), F32)]
                  + [jax.ShapeDtypeStruct(w.shape, BF16) for w in cast_weights],
        scratch_shapes=[pltpu.VMEM((FFN_TM, D), BF16)],
        compiler_params=_params("parallel", "arbitrary"),
        name="ffn_ln",
    )(x, w_in, w_in, w_out, g.reshape(1, D), b.reshape(1, D), *cast_weights)
    return outs[0], tuple(outs[1:])


def _proj_kernel(x_ref, w_ref, o_ref, xb_ref):
    @pl.when(pl.program_id(1) == 0)
    def _():
        xb_ref[...] = x_ref[...].astype(BF16)

    o_ref[...] = jnp.dot(xb_ref[...], w_ref[...], preferred_element_type=F32).astype(o_ref.dtype)


def _proj(x, w, out_dtype):
    S, D = x.shape
    N = w.shape[1]
    return pl.pallas_call(
        _proj_kernel,
        grid=(S // PROJ_TM, N // PROJ_TN),
        in_specs=[
            pl.BlockSpec((PROJ_TM, D), lambda i, j: (i, 0)),
            pl.BlockSpec((D, PROJ_TN), lambda i, j: (0, j)),
        ],
        out_specs=pl.BlockSpec((PROJ_TM, PROJ_TN), lambda i, j: (i, j)),
        out_shape=jax.ShapeDtypeStruct((S, N), out_dtype),
        scratch_shapes=[pltpu.VMEM((PROJ_TM, D), BF16)],
        compiler_params=_params("parallel", "arbitrary"),
        name="proj",
    )(x, w)


def _out_ln_kernel(x_ref, a_ref, w_ref, g_ref, b_ref, o_ref):
    k = pl.program_id(1)

    @pl.when(k == 0)
    def _():
        o_ref[...] = ALPHA * x_ref[...]

    _accumulate_dot(o_ref, a_ref[...], w_ref)

    @pl.when(k == pl.num_programs(1) - 1)
    def _():
        _layer_norm_inplace(o_ref, g_ref, b_ref)


def _out_ln(x, a, w, g, b):
    S, D = x.shape
    K = a.shape[1]
    return pl.pallas_call(
        _out_ln_kernel,
        grid=(S // OUT_TM, K // OUT_TK),
        in_specs=[
            pl.BlockSpec((OUT_TM, D), lambda i, k: (i, 0)),
            pl.BlockSpec((OUT_TM, OUT_TK), lambda i, k: (i, k)),
            pl.BlockSpec((OUT_TK, D), lambda i, k: (k, 0)),
            pl.BlockSpec((1, D), lambda i, k: (0, 0)),
            pl.BlockSpec((1, D), lambda i, k: (0, 0)),
        ],
        out_specs=pl.BlockSpec((OUT_TM, D), lambda i, k: (i, 0)),
        out_shape=jax.ShapeDtypeStruct((S, D), F32),
        compiler_params=_params("parallel", "arbitrary"),
        name="out_ln",
    )(x, a, w, g.reshape(1, D), b.reshape(1, D))


def _nt_dot(a, b):
    return lax.dot_general(a, b, (((1,), (1,)), ((), ())), preferred_element_type=F32)


def _tn_dot(a, b):
    return lax.dot_general(a, b, (((0,), (0,)), ((), ())), preferred_element_type=F32)


def _hgrn_kernel(q_ref, f_ref, v_ref, g_ref, lb_ref, ng_ref, o_ref,
                 st_ref, cum_ref, qs_ref, oi_ref):
    TB, W = q_ref.shape
    C = HG_CHUNK
    nchunk = TB // C
    nhead = W // HEAD_DIM

    @pl.when(pl.program_id(1) == 0)
    def _():
        st_ref[...] = jnp.zeros_like(st_ref)

    lg = lb_ref[...]
    ex = jnp.exp(lg - jnp.max(lg, axis=0, keepdims=True))
    lb = ex[0:1, :] / jnp.sum(ex, axis=0, keepdims=True)
    qv = q_ref[...]
    q = qv * jax.nn.sigmoid(qv)
    forget = lb + (1.0 - lb) * jax.nn.sigmoid(f_ref[...])
    k = 1.0 - forget
    logf = jnp.log(forget)

    r = lax.broadcasted_iota(jnp.int32, (TB, TB), 0)
    c = lax.broadcasted_iota(jnp.int32, (TB, TB), 1)
    tri = jnp.where((c <= r) & (c // C == r // C), 1.0, 0.0).astype(BF16)
    hi = logf.astype(BF16)
    r1 = logf - hi.astype(F32)
    mid = r1.astype(BF16)
    lo = (r1 - mid.astype(F32)).astype(BF16)
    cum = (jnp.dot(tri, hi, preferred_element_type=F32)
           + jnp.dot(tri, mid, preferred_element_type=F32)
           + jnp.dot(tri, lo, preferred_element_type=F32))

    bound = jnp.zeros((1, W), F32)
    for ci in range(nchunk):
        mrow = cum[ci * C + C // 2 - 1: ci * C + C // 2, :]
        lrow = cum[ci * C + C - 1: ci * C + C, :]
        bound = jnp.maximum(bound, jnp.maximum(-mrow, mrow - lrow))
    safe = jnp.max(bound) <= HG_SAFE_LOG

    tr = lax.broadcasted_iota(jnp.int32, (C, C), 0)
    ts = lax.broadcasted_iota(jnp.int32, (C, C), 1)
    causal = ts <= tr

    gv = g_ref[...]
    og = ng_ref[...] * (gv * jax.nn.sigmoid(gv))

    def factorised_intra(rows, lanes):
        cu = cum[rows, lanes]
        m = cu[C // 2 - 1: C // 2, :]
        qd = (q[rows, lanes] * jnp.exp(cu - m)).astype(BF16)
        kd = (k[rows, lanes] * jnp.exp(m - cu)).astype(BF16)
        a = jnp.where(causal, _nt_dot(qd, kd), 0.0).astype(BF16)
        return jnp.dot(a, v_ref[rows, lanes].astype(BF16), preferred_element_type=F32)

    def stored_intra(rows, lanes):
        return oi_ref[rows, lanes]

    def recurrence(intra):
        for h in range(nhead):
            lanes = slice(h * HEAD_DIM, (h + 1) * HEAD_DIM)
            st = st_ref[h]
            for ci in range(nchunk):
                rows = slice(ci * C, (ci + 1) * C)
                cu = cum[rows, lanes]
                last = cu[C - 1: C, :]
                qe = (q[rows, lanes] * jnp.exp(cu)).astype(BF16)
                o = intra(rows, lanes) + _nt_dot(qe, st.astype(BF16))
                ke = (k[rows, lanes] * jnp.exp(last - cu)).astype(BF16)
                st = st * jnp.exp(last) + _tn_dot(v_ref[rows, lanes].astype(BF16), ke)
                o = o * lax.rsqrt(jnp.mean(o * o, axis=-1, keepdims=True) + RMS_EPS)
                o_ref[rows, lanes] = (o * og[rows, lanes]).astype(o_ref.dtype)
            st_ref[h] = st

    @pl.when(safe)
    def _():
        recurrence(factorised_intra)

    @pl.when(jnp.logical_not(safe))
    def _():
        cum_ref[...] = cum
        qs_ref[...] = q
        sidx = lax.broadcasted_iota(jnp.int32, (C, W), 0)
        for ci in range(nchunk):
            rows = slice(ci * C, (ci + 1) * C)
            cu = cum[rows, :]
            kc = k[rows, :]
            vc = v_ref[rows, :]

            def rows_body(t8, carry, ci=ci, cu=cu, kc=kc, vc=vc):
                base = pl.multiple_of(ci * C + t8 * SUBLANES, SUBLANES)
                c8 = cum_ref[pl.ds(base, SUBLANES), :]
                q8 = qs_ref[pl.ds(base, SUBLANES), :]
                out_rows = []
                for r in range(SUBLANES):
                    t = t8 * SUBLANES + r
                    dec = jnp.exp(jnp.where(sidx <= t, c8[r:r + 1, :] - cu, -jnp.inf))
                    prod = q8[r:r + 1, :] * dec * kc
                    heads = []
                    for h in range(nhead):
                        lanes = slice(h * HEAD_DIM, (h + 1) * HEAD_DIM)
                        sc = jnp.sum(prod[:, lanes], axis=1, keepdims=True)
                        heads.append(jnp.sum(sc * vc[:, lanes], axis=0, keepdims=True))
                    out_rows.append(jnp.concatenate(heads, axis=1))
                oi_ref[pl.ds(base, SUBLANES), :] = jnp.concatenate(out_rows, axis=0)
                return carry

            lax.fori_loop(0, C // SUBLANES, rows_body, 0)
        recurrence(stored_intra)


def _hgrn_core(proj, lb_logits, norm_g):
    S, D4 = proj.shape
    D = D4 // 4
    W = HG_HB * HEAD_DIM
    ng = D // W
    return pl.pallas_call(
        _hgrn_kernel,
        grid=(ng, S // HG_TB),
        in_specs=[
            pl.BlockSpec((HG_TB, W), lambda hg, t: (t, hg)),
            pl.BlockSpec((HG_TB, W), lambda hg, t: (t, hg + ng)),
            pl.BlockSpec((HG_TB, W), lambda hg, t: (t, hg + 2 * ng)),
            pl.BlockSpec((HG_TB, W), lambda hg, t: (t, hg + 3 * ng)),
            pl.BlockSpec((lb_logits.shape[0], W), lambda hg, t: (0, hg)),
            pl.BlockSpec((1, W), lambda hg, t: (0, hg)),
        ],
        out_specs=pl.BlockSpec((HG_TB, W), lambda hg, t: (t, hg)),
        out_shape=jax.ShapeDtypeStruct((S, D), BF16),
        scratch_shapes=[
            pltpu.VMEM((HG_HB, HEAD_DIM, HEAD_DIM), F32),
            pltpu.VMEM((HG_TB, W), F32),
            pltpu.VMEM((HG_TB, W), F32),
            pltpu.VMEM((HG_TB, W), F32),
        ],
        compiler_params=_params("parallel", "arbitrary"),
        name="hgrn_core",
    )(proj, proj, proj, proj, lb_logits.astype(F32), norm_g.reshape(1, D))


def _moba_kernel(q_ref, k_ref, v_ref, o_ref, km_ref, sel_ref, s_ref, vt_ref, acc_ref):
    L = MB_BLOCK
    S = k_ref.shape[0]
    nb = S // L
    c2 = HEAD_DIM ** -0.5 * LOG2_E
    last_group = nb // MB_UNROLL - 1

    km_ref[...] = jnp.zeros_like(km_ref)

    def mean_body(n, carry):
        rows = pl.ds(pl.multiple_of(n * L, L), L)
        for h in range(MB_HEADS):
            lanes = slice(h * HEAD_DIM, (h + 1) * HEAD_DIM)
            km_ref[h, pl.ds(n + 1, 1), :] = jnp.mean(k_ref[rows, lanes].astype(F32), axis=0, keepdims=True)
            pad = (lax.broadcasted_iota(jnp.int32, (MB_VROWS - HEAD_DIM, L), 0) == 0).astype(BF16)
            vt_ref[h, n] = jnp.concatenate([v_ref[rows, lanes].astype(F32).T.astype(BF16), pad], axis=0)
        return carry
    lax.fori_loop(0, nb, mean_body, 0)

    def tile_body(i, carry):
        _moba_tile(i, q_ref, k_ref, vt_ref, o_ref, km_ref, sel_ref, s_ref, acc_ref, c2, last_group)
        return carry
    lax.fori_loop(0, nb, tile_body, 0)


def _moba_tile(i, q_ref, k_ref, vt_ref, o_ref, km_ref, sel_ref, s_ref, acc_ref, c2, last_group):
    L = MB_BLOCK
    nrow = sel_ref.shape[1]
    heads = range(MB_HEADS)
    lanes = [slice(h * HEAD_DIM, (h + 1) * HEAD_DIM) for h in heads]
    own = pl.ds(pl.multiple_of(i * L, L), L)
    qt32 = [q_ref[own, lanes[h]].astype(F32).T for h in heads]
    qt_raw = [x.astype(BF16) for x in qt32]
    qt = [(x * c2).astype(BF16) for x in qt32]

    ridx = lax.broadcasted_iota(jnp.int32, (nrow, L), 0)
    for h in heads:
        gate = jnp.dot(km_ref[h].astype(BF16), qt_raw[h], preferred_element_type=F32)
        gate = jnp.where((ridx >= 1) & (ridx <= i), gate, -jnp.inf)
        sel = jnp.where(ridx == 0, 1.0, 0.0)
        for _ in range(MB_TOPK):
            mx = jnp.max(gate, axis=0, keepdims=True)
            first = jnp.min(jnp.where(gate == mx, ridx, nrow), axis=0, keepdims=True)
            pick = (ridx == first) & (mx > -jnp.inf)
            sel = jnp.where(pick, 1.0, sel)
            gate = jnp.where(pick, -jnp.inf, gate)
        sel_ref[h] = sel

    kpos = lax.broadcasted_iota(jnp.int32, (L, L), 0)
    qpos = lax.broadcasted_iota(jnp.int32, (L, L), 1)

    def key_block(r):
        return jnp.where(r == 0, i, r - 1)

    def score_member(h, r, slot, u, colmax, causal):
        n = key_block(r)
        s = jnp.dot(k_ref[pl.ds(pl.multiple_of(n * L, L), L), lanes[h]], qt[h],
                    preferred_element_type=F32)
        if causal:
            s = jnp.where(kpos <= qpos, s, -jnp.inf)
        s_ref[h, slot, u] = s
        on = sel_ref[h, pl.ds(r, 1), :] > 0.0
        return jnp.maximum(colmax, jnp.where(on, jnp.max(s, axis=0, keepdims=True), -jnp.inf))

    def score_group(g, slot, first_group=False):
        colmax = [jnp.full((1, L), -jnp.inf, F32) for _ in heads]
        for h in heads:
            for u in range(MB_UNROLL):
                r = g * MB_UNROLL + u
                colmax[h] = score_member(h, r, slot, u, colmax[h], causal=first_group and u == 0)
        return colmax

    def attend_group(g, slot, colmax, m):
        m = list(m)
        for h in heads:
            m_new = jnp.maximum(m[h], colmax[h])
            ah = jnp.exp2(m[h] - m_new) * acc_ref[h]
            for u in range(MB_UNROLL):
                r = g * MB_UNROLL + u
                on = sel_ref[h, pl.ds(r, 1), :] > 0.0
                p = jnp.exp2(s_ref[h, slot, u] - jnp.where(on, m_new, jnp.inf))
                ah = ah + jnp.dot(vt_ref[h, key_block(r)], p.astype(BF16), preferred_element_type=F32)
            m[h] = m_new
            acc_ref[h] = ah
        return m

    def pair_body(pi, carry):
        m, cm_even = carry
        g = 2 * pi
        cm_odd = score_group(g + 1, 1)
        m = attend_group(g, 0, cm_even, m)
        cm_even = score_group(jnp.minimum(g + 2, last_group), 0)
        m = attend_group(g + 1, 1, cm_odd, m)
        return m, cm_even

    m0 = [jnp.full((1, L), MB_MAX_FLOOR, F32) for _ in heads]
    acc_ref[...] = jnp.zeros_like(acc_ref)
    ngroups = i // MB_UNROLL + 1
    m, cm_even = lax.fori_loop(0, ngroups // 2, pair_body,
                               (m0, score_group(0, 0, first_group=True)))

    @pl.when(ngroups % 2 == 1)
    def _():
        attend_group(ngroups - 1, 0, cm_even, m)

    for h in heads:
        o_ref[own, lanes[h]] = (acc_ref[h, 0:HEAD_DIM, :] / acc_ref[h, HEAD_DIM:HEAD_DIM + 1, :]
                                ).T.astype(o_ref.dtype)


def _moba_core(qkv):
    S, D3 = qkv.shape
    D = D3 // 3
    L = MB_BLOCK
    nb = S // L
    W = MB_HEADS * HEAD_DIM
    ng = D // W
    nrow = nb + SUBLANES
    assert nb % (2 * MB_UNROLL) == 0 and ng * W == D
    return pl.pallas_call(
        _moba_kernel,
        grid=(ng,),
        in_specs=[
            pl.BlockSpec((S, W), lambda hg: (0, hg)),
            pl.BlockSpec((S, W), lambda hg: (0, ng + hg)),
            pl.BlockSpec((S, W), lambda hg: (0, 2 * ng + hg)),
        ],
        out_specs=pl.BlockSpec((S, W), lambda hg: (0, hg)),
        out_shape=jax.ShapeDtypeStruct((S, D), BF16),
        scratch_shapes=[
            pltpu.VMEM((MB_HEADS, nrow, HEAD_DIM), F32),
            pltpu.VMEM((MB_HEADS, nrow, L), F32),
            pltpu.VMEM((MB_HEADS, 2, MB_UNROLL, L, L), F32),
            pltpu.VMEM((MB_HEADS, nb, MB_VROWS, L), BF16),
            pltpu.VMEM((MB_HEADS, MB_VROWS, L), F32),
        ],
        compiler_params=_params("parallel"),
        name="moba_core",
    )(qkv, qkv, qkv)


def kernel(x, lb_logits, l0_ffn1_in, l0_ffn1_out, l0_ln1_g, l0_ln1_b, l0_hg_in, l0_hg_norm_g, l0_hg_out, l0_ln2_g, l0_ln2_b, l0_ffn2_in, l0_ffn2_out, l0_ln3_g, l0_ln3_b, l1_ffn1_in, l1_ffn1_out, l1_ln1_g, l1_ln1_b, l1_mb_in, l1_mb_out, l1_ln2_g, l1_ln2_b, l1_ffn2_in, l1_ffn2_out, l1_ln3_g, l1_ln3_b):
    B, S, D = x.shape
    assert B == 1 and S % MB_BLOCK == 0 and S % FFN_TM == 0
    bf = lambda w: w.astype(BF16)
    h = x.reshape(S, D)

    h, (hg_in, hg_out, f2_in, f2_out) = _ffn_ln(
        h, bf(l0_ffn1_in), bf(l0_ffn1_out), l0_ln1_g, l0_ln1_b,
        cast_weights=(l0_hg_in, l0_hg_out, l0_ffn2_in, l0_ffn2_out))
    proj = _proj(h, hg_in, F32)
    mix = _hgrn_core(proj, lb_logits, l0_hg_norm_g)
    h = _out_ln(h, mix, hg_out, l0_ln2_g, l0_ln2_b)
    h, (f1_in, f1_out) = _ffn_ln(h, f2_in, f2_out, l0_ln3_g, l0_ln3_b,
                                 cast_weights=(l1_ffn1_in, l1_ffn1_out))

    h, (mb_in, mb_out, f2_in, f2_out) = _ffn_ln(
        h, f1_in, f1_out, l1_ln1_g, l1_ln1_b,
        cast_weights=(l1_mb_in, l1_mb_out, l1_ffn2_in, l1_ffn2_out))
    qkv = _proj(h, mb_in, BF16)
    mix = _moba_core(qkv)
    h = _out_ln(h, mix, mb_out, l1_ln2_g, l1_ln2_b)
    h, _ = _ffn_ln(h, f2_in, f2_out, l1_ln3_g, l1_ln3_b)
    return h.reshape(B, S, D)
```
